```python
import math
import jax, jax.numpy as jnp
from jax import lax
import numpy as np

D_MODEL = 1024
BATCH = 4
SEQ = 8192
DEPTH = 2

N_META = 16
SSD_HEADS = 16
SSD_HEAD_DIM = 64
SSD_WIDTH = SSD_HEADS * SSD_HEAD_DIM
SSD_GROUPS = 2
SSD_HEADS_PER_GROUP = SSD_HEADS // SSD_GROUPS
SSD_STATE = 128
SSD_CONV = 4
CHUNK = 128
CM_WIDTH = 1024
CM_CONV_WIDTH = 31
MIX_WIDTH = SSD_WIDTH + CM_WIDTH
XBC_WIDTH = SSD_WIDTH + 2 * SSD_GROUPS * SSD_STATE
IN_WIDTH = SSD_WIDTH + XBC_WIDTH + SSD_HEADS + 2 * CM_WIDTH
D_FF = 2816
N_EXPERTS = 8
TOP_K = 2
D_FF_EXPERT = 3584
MOE_BLOCK = 256
N_DENSE = (DEPTH + 1) // 2
N_MOE = DEPTH // 2
RMS_EPS = 1e-6
LN_EPS = 1e-5

kernel_name = "hymba_ssd_conformer_moe_block"


def rms_norm(t, g):
    t32 = t.astype(jnp.float32)
    y = t32 * lax.rsqrt(jnp.mean(t32 * t32, axis=-1, keepdims=True) + RMS_EPS)
    return (y * g.astype(jnp.float32)).astype(t.dtype)


def layer_norm(t, g, b):
    t32 = t.astype(jnp.float32)
    mu = jnp.mean(t32, axis=-1, keepdims=True)
    c = t32 - mu
    y = c * lax.rsqrt(jnp.mean(c * c, axis=-1, keepdims=True) + LN_EPS)
    return (y * g.astype(jnp.float32) + b.astype(jnp.float32)).astype(t.dtype)


def causal_dwconv(t, w, b):
    k, c = w.shape
    out = lax.conv_general_dilated(
        t, w[:, None, :].astype(t.dtype), window_strides=(1,),
        padding=((k - 1, 0),), dimension_numbers=("NWC", "WIO", "NWC"),
        feature_group_count=c)
    return out + b.astype(t.dtype)


def ssd_group(z, xbc, dt_raw, conv_w, conv_b, dt_bias, a_log, d_skip, norm_g):
    bsz, seqlen, _ = xbc.shape
    xbc = jax.nn.silu(causal_dwconv(xbc, conv_w, conv_b))
    xs = xbc[..., :SSD_WIDTH]
    bm = xbc[..., SSD_WIDTH:SSD_WIDTH + SSD_GROUPS * SSD_STATE]
    cm = xbc[..., SSD_WIDTH + SSD_GROUPS * SSD_STATE:]
    dt = jax.nn.softplus(dt_raw.astype(jnp.float32) + dt_bias.astype(jnp.float32))
    a = -jnp.exp(a_log.astype(jnp.float32))
    pad = (-N_META) % CHUNK
    padf = lambda t: jnp.pad(t, ((0, 0), (pad, 0), (0, 0)))
    xs, bm, cm, dt = padf(xs), padf(bm), padf(cm), padf(dt)
    lp = seqlen + pad
    nc = lp // CHUNK
    G, E, P, N = SSD_GROUPS, SSD_HEADS_PER_GROUP, SSD_HEAD_DIM, SSD_STATE
    X = xs.reshape(bsz, nc, CHUNK, G, E, P)
    Bc = bm.reshape(bsz, nc, CHUNK, G, N)
    Cc = cm.reshape(bsz, nc, CHUNK, G, N)
    dtc = dt.reshape(bsz, nc, CHUNK, G, E)
    dA = dtc * a.reshape(G, E)
    cs = jnp.cumsum(dA, axis=2)
    Xdt = X * dtc[..., None]
    cs_t = jnp.moveaxis(cs, 2, -1)
    seg = cs_t[..., :, None] - cs_t[..., None, :]
    causal = jnp.tril(jnp.ones((CHUNK, CHUNK), dtype=bool))
    decay = jnp.exp(jnp.where(causal, seg, -jnp.inf))
    cb = jnp.einsum("bclgn,bcsgn->bcgls", Cc, Bc)
    y_diag = jnp.einsum("bcgls,bcgels,bcsgep->bclgep", cb, decay, Xdt)
    decay_to_end = jnp.exp(cs[:, :, -1:] - cs)
    states = jnp.einsum("bclgn,bclge,bclgep->bcgepn", Bc, decay_to_end, Xdt)
    chunk_decay = jnp.exp(cs[:, :, -1])

    def step(carry, inp):
        st, dec = inp
        return carry * dec[..., None, None] + st, carry

    init = jnp.zeros((bsz, G, E, P, N), dtype=states.dtype)
    _, prev = lax.scan(step, init, (jnp.moveaxis(states, 1, 0), jnp.moveaxis(chunk_decay, 1, 0)))
    prev = jnp.moveaxis(prev, 0, 1)
    y_off = jnp.einsum("bclgn,bcgepn,bclge->bclgep", Cc, prev, jnp.exp(cs))
    y = y_diag + y_off + X * d_skip.reshape(G, E)[:, :, None]
    y = y.reshape(bsz, lp, SSD_WIDTH)[:, pad:]
    y = y * jax.nn.silu(z.astype(y.dtype))
    return rms_norm(y, norm_g).astype(z.dtype)


def conformer_group(cm_in, pw_b, dw_w, dw_b, ln_g, ln_b):
    a = cm_in + pw_b.astype(cm_in.dtype)
    val, gate = a[..., :CM_WIDTH], a[..., CM_WIDTH:]
    g = val * jax.nn.sigmoid(gate)
    g = causal_dwconv(g, dw_w, dw_b)
    g = layer_norm(g, ln_g, ln_b)
    return jax.nn.silu(g)


def swiglu(t, wg, wu, wd):
    return (jax.nn.silu(t @ wg) * (t @ wu)) @ wd


def moe_swiglu(h2, router_w, w_gate, w_up, w_down):
    T = h2.shape[0]
    logits = jnp.dot(h2.astype(jnp.float32), router_w.astype(jnp.float32))
    top_logit, top_e = lax.top_k(logits, TOP_K)
    gate = jax.nn.softmax(top_logit, axis=-1)
    n_assign = T * TOP_K
    flat_e = top_e.reshape(-1).astype(jnp.int32)
    flat_tok = jnp.repeat(jnp.arange(T, dtype=jnp.int32), TOP_K)
    flat_g = gate.reshape(-1)
    order = jnp.argsort(flat_e)
    sorted_e = flat_e[order]
    counts = jnp.bincount(flat_e, length=N_EXPERTS)
    padded = (counts + MOE_BLOCK - 1) // MOE_BLOCK * MOE_BLOCK
    seg_start = jnp.cumsum(counts) - counts
    pad_end = jnp.cumsum(padded)
    pad_start = pad_end - padded
    dest = pad_start[sorted_e] + jnp.arange(n_assign, dtype=jnp.int32) - seg_start[sorted_e]
    n_blocks = -(-n_assign // MOE_BLOCK) + N_EXPERTS
    n_rows = n_blocks * MOE_BLOCK
    row_tok = jnp.zeros((n_rows,), jnp.int32).at[dest].set(flat_tok[order])
    row_gate = jnp.zeros((n_rows,), jnp.float32).at[dest].set(flat_g[order])
    block_e = jnp.minimum(
        jnp.searchsorted(pad_end, jnp.arange(n_blocks) * MOE_BLOCK, side="right"),
        N_EXPERTS - 1).astype(jnp.int32)

    def block_fn(args):
        tok, g, e = args
        xb = h2[tok]
        out = swiglu(xb, w_gate[e], w_up[e], w_down[e])
        return out * g[:, None].astype(out.dtype)

    rows_out = lax.map(block_fn, (row_tok.reshape(n_blocks, MOE_BLOCK),
                                  row_gate.reshape(n_blocks, MOE_BLOCK), block_e))
    rows_out = rows_out.reshape(n_rows, h2.shape[-1]).astype(h2.dtype)
    return jnp.zeros_like(h2).at[row_tok].add(rows_out)


def setup_inputs(seed: int = 0) -> dict:
    key = jax.random.key(seed)
    ks = jax.random.split(key, 26)
    f32 = jnp.float32
    nrm = lambda k, shape, scale: jax.random.normal(k, shape, f32) * scale
    dt0 = jnp.exp(jax.random.uniform(ks[6], (DEPTH, SSD_HEADS), f32,
                                     math.log(1e-3), math.log(1e-1)))
    return {
        "x": nrm(ks[0], (BATCH, SEQ, D_MODEL), 1.0),
        "meta_tokens": nrm(ks[1], (N_META, D_MODEL), 1.0),
        "norm_mix_g": 1.0 + nrm(ks[2], (DEPTH, D_MODEL), 0.02),
        "w_in": nrm(ks[3], (DEPTH, D_MODEL, IN_WIDTH), D_MODEL ** -0.5),
        "ssd_conv_w": nrm(ks[4], (DEPTH, SSD_CONV, XBC_WIDTH), SSD_CONV ** -0.5),
        "ssd_conv_b": nrm(ks[5], (DEPTH, XBC_WIDTH), 0.01),
        "ssd_dt_bias": dt0 + jnp.log(-jnp.expm1(-dt0)),
        "ssd_a_log": jnp.log(jax.random.uniform(ks[7], (DEPTH, SSD_HEADS), f32, 1.0, 16.0)),
        "ssd_d": 1.0 + nrm(ks[8], (DEPTH, SSD_HEADS), 0.1),
        "ssd_norm_g": 1.0 + nrm(ks[9], (DEPTH, SSD_WIDTH), 0.02),
        "cm_pw_b": nrm(ks[10], (DEPTH, 2 * CM_WIDTH), 0.01),
        "cm_dw_w": nrm(ks[11], (DEPTH, CM_CONV_WIDTH, CM_WIDTH), CM_CONV_WIDTH ** -0.5),
        "cm_dw_b": nrm(ks[12], (DEPTH, CM_WIDTH), 0.01),
        "cm_ln_g": 1.0 + nrm(ks[13], (DEPTH, CM_WIDTH), 0.02),
        "cm_ln_b": nrm(ks[14], (DEPTH, CM_WIDTH), 0.01),
        "w_out": nrm(ks[15], (DEPTH, MIX_WIDTH, D_MODEL), MIX_WIDTH ** -0.5),
        "norm_ffn_g": 1.0 + nrm(ks[16], (DEPTH, D_MODEL), 0.02),
        "ffn_w_gate": nrm(ks[17], (N_DENSE, D_MODEL, D_FF), D_MODEL ** -0.5),
        "ffn_w_up": nrm(ks[18], (N_DENSE, D_MODEL, D_FF), D_MODEL ** -0.5),
        "ffn_w_down": nrm(ks[19], (N_DENSE, D_FF, D_MODEL), D_FF ** -0.5),
        "moe_router": nrm(ks[20], (N_MOE, D_MODEL, N_EXPERTS), D_MODEL ** -0.5),
        "moe_w_gate": nrm(ks[21], (N_MOE, N_EXPERTS, D_MODEL, D_FF_EXPERT), D_MODEL ** -0.5),
        "moe_w_up": nrm(ks[22], (N_MOE, N_EXPERTS, D_MODEL, D_FF_EXPERT), D_MODEL ** -0.5),
        "moe_w_down": nrm(ks[23], (N_MOE, N_EXPERTS, D_FF_EXPERT, D_MODEL), D_FF_EXPERT ** -0.5),
        "final_norm_g": 1.0 + nrm(ks[24], (D_MODEL,), 0.02),
    }


def reference(x, meta_tokens, norm_mix_g, w_in, ssd_conv_w, ssd_conv_b, ssd_dt_bias,
              ssd_a_log, ssd_d, ssd_norm_g, cm_pw_b, cm_dw_w, cm_dw_b, cm_ln_g, cm_ln_b,
              w_out, norm_ffn_g, ffn_w_gate, ffn_w_up, ffn_w_down, moe_router,
              moe_w_gate, moe_w_up, moe_w_down, final_norm_g):
    bsz = x.shape[0]
    meta = jnp.broadcast_to(meta_tokens[None].astype(x.dtype), (bsz, N_META, D_MODEL))
    h = jnp.concatenate([meta, x], axis=1)
    o_xbc = SSD_WIDTH
    o_dt = o_xbc + XBC_WIDTH
    o_cm = o_dt + SSD_HEADS
    for layer in range(DEPTH):
        u = rms_norm(h, norm_mix_g[layer])
        proj = u @ w_in[layer]
        y_ssd = ssd_group(proj[..., :o_xbc], proj[..., o_xbc:o_dt], proj[..., o_dt:o_cm],
                          ssd_conv_w[layer], ssd_conv_b[layer], ssd_dt_bias[layer],
                          ssd_a_log[layer], ssd_d[layer], ssd_norm_g[layer])
        y_cm = conformer_group(proj[..., o_cm:], cm_pw_b[layer], cm_dw_w[layer],
                               cm_dw_b[layer], cm_ln_g[layer], cm_ln_b[layer])
        mixed = jnp.concatenate([y_ssd.astype(h.dtype), y_cm.astype(h.dtype)], axis=-1)
        h = h + mixed @ w_out[layer]
        v = rms_norm(h, norm_ffn_g[layer])
        if layer % 2 == 0:
            i = layer // 2
            f = swiglu(v, ffn_w_gate[i], ffn_w_up[i], ffn_w_down[i])
        else:
            i = layer // 2
            f = moe_swiglu(v.reshape(-1, D_MODEL), moe_router[i], moe_w_gate[i],
                           moe_w_up[i], moe_w_down[i]).reshape(v.shape)
        h = h + f.astype(h.dtype)
    return rms_norm(h[:, N_META:], final_norm_g)
```

```python
import functools

import jax
import jax.numpy as jnp
from jax import lax
from jax.experimental import pallas as pl
from jax.experimental.pallas import tpu as pltpu

F32 = jnp.float32
BF16 = jnp.bfloat16

CHUNK = 128
SSD_GROUPS = 2
SSD_STATE = 128
TOP_K = 2
RMS_EPS = 1e-6
LN_EPS = 1e-5
LANES = 128
SUBLANES = 8
DT_COPIES = 3
VMEM_LIMIT = 56 * 1024 * 1024


def _largest_divisor(n, candidates):
    for c in candidates:
        if n % c == 0:
            return c
    raise ValueError(f"no tile in {candidates} divides {n}")


def _const_spec(shape):
    nd = len(shape)
    return pl.BlockSpec(shape, lambda *_: (0,) * nd, pipeline_mode=pl.Buffered(1))


def _rms(x, g):
    return x * lax.rsqrt(jnp.mean(x * x, axis=-1, keepdims=True) + RMS_EPS) * g


def _silu(x):
    return x * jax.nn.sigmoid(x)


def _bdot(a, b):
    return jnp.dot(a, b, preferred_element_type=F32)


def _in_proj_body(h_ref, g_ref, w_ref, z_ref, xbc_ref, cm_ref, dt_ref, *, widths):
    wz, wx, wc, wd = widths
    u = _rms(h_ref[...], g_ref[...]).astype(BF16)
    o = 0
    z_ref[...] = _bdot(u, w_ref[:, o:o + wz]).astype(BF16)
    o += wz
    xbc_ref[...] = _bdot(u, w_ref[:, o:o + wx]).astype(BF16)
    o += wx
    cm_ref[...] = _bdot(u, w_ref[:, o:o + wc]).astype(BF16)
    o += wc
    dt_ref[...] = _bdot(u, w_ref[:, o:o + wd])


def _in_proj(h, g, w, widths):
    rows, d = h.shape
    tm = _largest_divisor(rows, (512, 256, 128))
    wz, wx, wc, wd = widths
    row_spec = lambda n: pl.BlockSpec((tm, n), lambda i: (i, 0))
    return pl.pallas_call(
        functools.partial(_in_proj_body, widths=widths),
        grid=(rows // tm,),
        in_specs=[row_spec(d), _const_spec((1, d)), _const_spec(w.shape)],
        out_specs=[row_spec(wz), row_spec(wx), row_spec(wc), row_spec(wd)],
        out_shape=[jax.ShapeDtypeStruct((rows, wz), BF16),
                   jax.ShapeDtypeStruct((rows, wx), BF16),
                   jax.ShapeDtypeStruct((rows, wc), BF16),
                   jax.ShapeDtypeStruct((rows, wd), F32)],
        compiler_params=pltpu.CompilerParams(
            dimension_semantics=("parallel",), vmem_limit_bytes=VMEM_LIMIT),
        name="in_proj",
    )(h, g, w)


def _ssd_body(xbc_ref, dt_ref, z_ref, cw_ref, cb_ref, dtb_ref, a_ref, dsk_ref, ng_ref, e3_ref,
              y_ref, carry_ref, state_ref, *, pad, heads, head_dim, conv_k):
    c = pl.program_id(1)
    width = heads * head_dim
    gw = width // SSD_GROUPS
    n = SSD_STATE

    @pl.when(c == 0)
    def _():
        carry_ref[...] = jnp.zeros_like(carry_ref)
        state_ref[...] = jnp.zeros_like(state_ref)

    row = c * CHUNK + lax.broadcasted_iota(jnp.int32, (CHUNK, 1), 0)
    live = row >= pad

    cur = jnp.where(live, xbc_ref[...].astype(F32), 0.0)
    ext = jnp.concatenate([carry_ref[...], cur], axis=0)
    carry_ref[...] = cur[CHUNK - SUBLANES:, :]
    acc = cb_ref[...] + cw_ref[conv_k - 1:conv_k, :] * cur
    for k in range(conv_k - 1):
        shifted = pltpu.roll(ext, conv_k - 1 - k, 0)[SUBLANES:, :]
        acc = acc + cw_ref[k:k + 1, :] * shifted
    xbc = _silu(acc)
    xs = xbc[:, :width]
    bm = xbc[:, width:width + SSD_GROUPS * n].astype(BF16)
    cm = xbc[:, width + SSD_GROUPS * n:].astype(BF16)

    dt = jnp.where(live, jax.nn.softplus(dt_ref[...] + dtb_ref[...]), 0.0)
    da = dt * a_ref[...]
    li = lax.broadcasted_iota(jnp.int32, (CHUNK, CHUNK), 0)
    si = lax.broadcasted_iota(jnp.int32, (CHUNK, CHUNK), 1)
    causal = li >= si
    cs = jnp.dot(causal.astype(F32), da, precision=lax.Precision.HIGHEST,
                 preferred_element_type=F32)
    cs_t = cs.T
    total = cs[CHUNK - 1:CHUNK, :]

    lane = lax.broadcasted_iota(jnp.int32, (1, LANES), 1)

    def expand(v):
        hi = v.astype(BF16).astype(F32)
        r1 = v - hi
        mid = r1.astype(BF16).astype(F32)
        lo = r1 - mid
        parts = jnp.where(lane < heads, hi,
                          jnp.where(lane < 2 * heads, mid,
                                    jnp.where(lane < 3 * heads, lo, 0.0)))
        return _bdot(parts.astype(BF16), e3_ref[...])

    dt_x = expand(dt)
    ecs_x = expand(jnp.exp(cs))
    dte_x = expand(jnp.exp(total - cs))
    cdec_x = expand(jnp.broadcast_to(jnp.exp(total), (SUBLANES, LANES)))[0:1, :]

    xdt = xs * dt_x
    xdt_b = xdt.astype(BF16)
    xend_b = (xdt * dte_x).astype(BF16)
    half = lax.broadcasted_iota(jnp.int32, (1, LANES), 1) < head_dim

    hpg = heads // SSD_GROUPS
    y_parts = []
    for g in range(SSD_GROUPS):
        cg = cm[:, g * n:(g + 1) * n]
        bg = bm[:, g * n:(g + 1) * n]
        cb = lax.dot_general(cg, bg, (((1,), (1,)), ((), ())), preferred_element_type=F32)
        for j in range(hpg // 2):
            ms = []
            for e in (g * hpg + 2 * j, g * hpg + 2 * j + 1):
                seg = cs[:, e:e + 1] - cs_t[e:e + 1, :]
                dec = jnp.exp(jnp.where(causal, seg, -jnp.inf))
                ms.append((cb * dec).astype(BF16))
            lhs = jnp.concatenate(ms, axis=1)
            lo_ch = (g * hpg + 2 * j) * head_dim
            x2 = xdt_b[:, lo_ch:lo_ch + LANES]
            zero = jnp.zeros_like(x2)
            rhs = jnp.concatenate([jnp.where(half, x2, zero), jnp.where(half, zero, x2)], axis=0)
            y_parts.append(_bdot(lhs, rhs))
    y = jnp.concatenate(y_parts, axis=1)

    off_parts = []
    for g in range(SSD_GROUPS):
        cg = cm[:, g * n:(g + 1) * n]
        bg = bm[:, g * n:(g + 1) * n]
        sg = state_ref[:, g * gw:(g + 1) * gw]
        off_parts.append(_bdot(cg, sg.astype(BF16)))
        st = lax.dot_general(bg, xend_b[:, g * gw:(g + 1) * gw], (((0,), (0,)), ((), ())),
                             preferred_element_type=F32)
        state_ref[:, g * gw:(g + 1) * gw] = sg * cdec_x[:, g * gw:(g + 1) * gw] + st
    y = y + jnp.concatenate(off_parts, axis=1) * ecs_x + xs * dsk_ref[...]

    y = y * _silu(z_ref[...].astype(F32))
    y_ref[...] = _rms(y, ng_ref[...]).astype(BF16)


def _ssd(xbc, dt, z, conv_w, conv_b, dt_bias, a, d_skip_x, norm_g, e3, *, batch, pad, heads, head_dim):
    rows, xw = xbc.shape
    width = heads * head_dim
    nc = rows // batch // CHUNK
    conv_k = conv_w.shape[0]
    blk = lambda n: pl.BlockSpec((CHUNK, n), lambda b, c: (b * nc + c, 0))
    return pl.pallas_call(
        functools.partial(_ssd_body, pad=pad, heads=heads, head_dim=head_dim, conv_k=conv_k),
        grid=(batch, nc),
        in_specs=[blk(xw), blk(LANES), blk(width),
                  _const_spec(conv_w.shape), _const_spec((1, xw)), _const_spec((1, LANES)),
                  _const_spec((1, LANES)), _const_spec((1, width)), _const_spec((1, width)),
                  _const_spec(e3.shape)],
        out_specs=blk(width),
        out_shape=jax.ShapeDtypeStruct((rows, width), BF16),
        scratch_shapes=[pltpu.VMEM((SUBLANES, xw), F32), pltpu.VMEM((SSD_STATE, width), F32)],
        compiler_params=pltpu.CompilerParams(
            dimension_semantics=("parallel", "arbitrary"), vmem_limit_bytes=VMEM_LIMIT),
        name="ssd",
    )(xbc, dt, z, conv_w, conv_b, dt_bias, a, d_skip_x, norm_g, e3)


def _cm_body(cm_ref, pwb_ref, w_ref, b_ref, lng_ref, lnb_ref, y_ref, carry_ref, *, pad, tm, conv_k, halo):
    t = pl.program_id(1)
    width = y_ref.shape[1]

    @pl.when(t == 0)
    def _():
        carry_ref[...] = jnp.zeros_like(carry_ref)

    row = t * tm + lax.broadcasted_iota(jnp.int32, (tm, 1), 0)
    a = cm_ref[...].astype(F32) + pwb_ref[...]
    glu = a[:, :width] * jax.nn.sigmoid(a[:, width:])
    glu = jnp.where(row >= pad, glu, 0.0)
    ext = jnp.concatenate([carry_ref[...], glu], axis=0)
    carry_ref[...] = glu[tm - halo:, :]

    acc = jnp.broadcast_to(b_ref[...], (tm, width))
    for r in range(SUBLANES):
        rolled = ext if r == 0 else pltpu.roll(ext, r, 0)
        for q in range((conv_k - 1 - r) // SUBLANES + 1):
            s = SUBLANES * q + r
            lo = halo - SUBLANES * q
            acc = acc + w_ref[conv_k - 1 - s:conv_k - s, :] * rolled[lo:lo + tm, :]

    mu = jnp.mean(acc, axis=-1, keepdims=True)
    cen = acc - mu
    y = cen * lax.rsqrt(jnp.mean(cen * cen, axis=-1, keepdims=True) + LN_EPS)
    y = y * lng_ref[...] + lnb_ref[...]
    y_ref[...] = _silu(y).astype(BF16)


def _conformer(cm, pw_b, dw_w, dw_b, ln_g, ln_b, *, batch, pad):
    rows, w2 = cm.shape
    width = w2 // 2
    conv_k = dw_w.shape[0]
    halo = -(-(conv_k - 1) // SUBLANES) * SUBLANES
    lp = rows // batch
    tm = _largest_divisor(lp, (128,))
    nt = lp // tm
    blk = lambda n: pl.BlockSpec((tm, n), lambda b, t: (b * nt + t, 0))
    return pl.pallas_call(
        functools.partial(_cm_body, pad=pad, tm=tm, conv_k=conv_k, halo=halo),
        grid=(batch, nt),
        in_specs=[blk(w2), _const_spec((1, w2)), _const_spec(dw_w.shape), _const_spec((1, width)),
                  _const_spec((1, width)), _const_spec((1, width))],
        out_specs=blk(width),
        out_shape=jax.ShapeDtypeStruct((rows, width), BF16),
        scratch_shapes=[pltpu.VMEM((halo, width), F32)],
        compiler_params=pltpu.CompilerParams(
            dimension_semantics=("parallel", "arbitrary"), vmem_limit_bytes=VMEM_LIMIT),
        name="conformer",
    )(cm, pw_b, dw_w, dw_b, ln_g, ln_b)


def _out_ffn_body(h_ref, ys_ref, yc_ref, wo_ref, g_ref, wg_ref, wu_ref, wd_ref, o_ref):
    ws = ys_ref.shape[1]
    h = h_ref[...] + _bdot(ys_ref[...], wo_ref[:ws, :]) + _bdot(yc_ref[...], wo_ref[ws:, :])
    v = _rms(h, g_ref[...]).astype(BF16)
    act = (_silu(_bdot(v, wg_ref[...])) * _bdot(v, wu_ref[...])).astype(BF16)
    o_ref[...] = h + _bdot(act, wd_ref[...])


def _out_ffn(h, ys, yc, w_out, g, wg, wu, wd):
    rows, d = h.shape
    tm = _largest_divisor(rows, (512, 256, 128))
    row_spec = lambda n: pl.BlockSpec((tm, n), lambda i: (i, 0))
    return pl.pallas_call(
        _out_ffn_body,
        grid=(rows // tm,),
        in_specs=[row_spec(d), row_spec(ys.shape[1]), row_spec(yc.shape[1]), _const_spec(w_out.shape),
                  _const_spec((1, d)), _const_spec(wg.shape), _const_spec(wu.shape), _const_spec(wd.shape)],
        out_specs=row_spec(d),
        out_shape=jax.ShapeDtypeStruct((rows, d), F32),
        compiler_params=pltpu.CompilerParams(
            dimension_semantics=("parallel",), vmem_limit_bytes=VMEM_LIMIT),
        name="out_ffn",
    )(h, ys, yc, w_out, g, wg, wu, wd)


def _out_router_body(h_ref, ys_ref, yc_ref, wo_ref, g_ref, wr_ref,
                     ho_ref, v_ref, idx_ref, gate_ref, cnt_ref, run_ref, *, n_experts):
    i = pl.program_id(0)
    tm = h_ref.shape[0]
    ws = ys_ref.shape[1]

    @pl.when(i == 0)
    def _():
        run_ref[...] = jnp.zeros_like(run_ref)

    h = h_ref[...] + _bdot(ys_ref[...], wo_ref[:ws, :]) + _bdot(yc_ref[...], wo_ref[ws:, :])
    ho_ref[...] = h
    v = _rms(h, g_ref[...])
    v_ref[...] = v

    logits = jnp.dot(v, wr_ref[...], precision=lax.Precision.HIGHEST, preferred_element_type=F32)
    lane = lax.broadcasted_iota(jnp.int32, (tm, LANES), 1)
    neg = -jnp.inf
    l1 = jnp.where(lane < n_experts, logits, neg)
    m1 = jnp.max(l1, axis=-1, keepdims=True)
    i1 = jnp.min(jnp.where(l1 == m1, lane, LANES), axis=-1, keepdims=True)
    l2 = jnp.where(lane == i1, neg, l1)
    m2 = jnp.max(l2, axis=-1, keepdims=True)
    i2 = jnp.min(jnp.where(l2 == m2, lane, LANES), axis=-1, keepdims=True)
    e21 = jnp.exp(m2 - m1)
    g1 = 1.0 / (1.0 + e21)
    g2 = e21 / (1.0 + e21)

    hot1 = lane == i1
    hot2 = lane == i2
    hot = jnp.where(hot1 | hot2, 1.0, 0.0)
    ri = lax.broadcasted_iota(jnp.int32, (tm, tm), 0)
    ci = lax.broadcasted_iota(jnp.int32, (tm, tm), 1)
    before = _bdot((ri > ci).astype(BF16), hot.astype(BF16)) + run_ref[...]
    p1 = jnp.sum(jnp.where(hot1, before, 0.0), axis=-1, keepdims=True).astype(jnp.int32)
    p2 = jnp.sum(jnp.where(hot2, before, 0.0), axis=-1, keepdims=True).astype(jnp.int32)
    run_ref[...] = run_ref[...] + jnp.sum(hot, axis=0, keepdims=True)
    cnt_ref[...] = jnp.broadcast_to(run_ref[...], cnt_ref.shape)

    idx_ref[...] = jnp.where(lane == 0, i1, jnp.where(lane == 1, i2, jnp.where(lane == 2, p1,
                             jnp.where(lane == 3, p2, 0))))
    gate_ref[...] = jnp.where(lane == 0, g1, jnp.where(lane == 1, g2, 0.0))


def _out_router(h, ys, yc, w_out, g, w_router, n_experts):
    rows, d = h.shape
    tm = _largest_divisor(rows, (512, 256, 128))
    row_spec = lambda n: pl.BlockSpec((tm, n), lambda i: (i, 0))
    return pl.pallas_call(
        functools.partial(_out_router_body, n_experts=n_experts),
        grid=(rows // tm,),
        in_specs=[row_spec(d), row_spec(ys.shape[1]), row_spec(yc.shape[1]), _const_spec(w_out.shape),
                  _const_spec((1, d)), _const_spec(w_router.shape)],
        out_specs=[row_spec(d), row_spec(d), row_spec(LANES), row_spec(LANES),
                   pl.BlockSpec((SUBLANES, LANES), lambda i: (0, 0))],
        out_shape=[jax.ShapeDtypeStruct((rows, d), F32), jax.ShapeDtypeStruct((rows, d), F32),
                   jax.ShapeDtypeStruct((rows, LANES), jnp.int32),
                   jax.ShapeDtypeStruct((rows, LANES), F32),
                   jax.ShapeDtypeStruct((SUBLANES, LANES), F32)],
        scratch_shapes=[pltpu.VMEM((1, LANES), F32)],
        compiler_params=pltpu.CompilerParams(
            dimension_semantics=("arbitrary",), vmem_limit_bytes=VMEM_LIMIT),
        name="out_router",
    )(h, ys, yc, w_out, g, w_router)


def _dispatch_body(dest_ref, v_ref, init_ref, xs_ref, sem, *, tm):
    del init_ref
    base = pl.program_id(0) * tm

    def copy(r, k):
        return pltpu.make_async_copy(v_ref.at[pl.ds(base + r, 1)],
                                     xs_ref.at[pl.ds(dest_ref[0, 0, TOP_K * r + k], 1)], sem)

    def start(r, carry):
        for k in range(TOP_K):
            copy(r, k).start()
        return carry

    def wait(r, carry):
        for k in range(TOP_K):
            copy(r, k).wait()
        return carry

    lax.fori_loop(0, tm, start, 0)
    lax.fori_loop(0, tm, wait, 0)


def _dispatch(v, dest, n_slots):
    rows, d = v.shape
    tm = _largest_divisor(rows, (512, 256, 128))
    nt = rows // tm
    dest3 = dest.reshape(nt, 1, TOP_K * tm)
    return pl.pallas_call(
        functools.partial(_dispatch_body, tm=tm),
        grid=(nt,),
        in_specs=[pl.BlockSpec((1, 1, TOP_K * tm), lambda i: (i, 0, 0), memory_space=pltpu.SMEM),
                  pl.BlockSpec(memory_space=pl.ANY), pl.BlockSpec(memory_space=pl.ANY)],
        out_specs=pl.BlockSpec(memory_space=pl.ANY),
        out_shape=jax.ShapeDtypeStruct((n_slots, d), F32),
        scratch_shapes=[pltpu.SemaphoreType.DMA(())],
        input_output_aliases={2: 0},
        compiler_params=pltpu.CompilerParams(dimension_semantics=("arbitrary",)),
        name="dispatch",
    )(dest3, v, jnp.zeros((n_slots, d), F32))


def _gmm_body(te_ref, nu_ref, x_ref, wg_ref, wu_ref, wd_ref, y_ref, xb_ref, acc_ref):
    i = pl.program_id(0)
    j = pl.program_id(1)
    last = pl.num_programs(1) - 1

    @pl.when(i < nu_ref[0])
    def _():
        @pl.when(j == 0)
        def _():
            xb_ref[...] = x_ref[...].astype(BF16)
            acc_ref[...] = jnp.zeros_like(acc_ref)

        xb = xb_ref[...]
        act = (_silu(_bdot(xb, wg_ref[0])) * _bdot(xb, wu_ref[0])).astype(BF16)
        acc_ref[...] += _bdot(act, wd_ref[0])

        @pl.when(j == last)
        def _():
            y_ref[...] = acc_ref[...]

    @pl.when((i >= nu_ref[0]) & (j == last))
    def _():
        y_ref[...] = jnp.zeros_like(y_ref)


def _gmm(xs, tile_expert, n_used, wg, wu, wd, tm):
    n_slots, d = xs.shape
    dff = wg.shape[2]
    fc = _largest_divisor(dff, (896, 512, 256, 128))
    nt = n_slots // tm
    nff = dff // fc
    tile = lambda i, nu: jnp.minimum(i, nu[0] - 1)
    chunk = lambda i, j, nu: jnp.where(i < nu[0], j, nff - 1)
    grid_spec = pltpu.PrefetchScalarGridSpec(
        num_scalar_prefetch=2,
        grid=(nt, nff),
        in_specs=[pl.BlockSpec((tm, d), lambda i, j, te, nu: (tile(i, nu), 0)),
                  pl.BlockSpec((1, d, fc), lambda i, j, te, nu: (te[tile(i, nu)], 0, chunk(i, j, nu))),
                  pl.BlockSpec((1, d, fc), lambda i, j, te, nu: (te[tile(i, nu)], 0, chunk(i, j, nu))),
                  pl.BlockSpec((1, fc, d), lambda i, j, te, nu: (te[tile(i, nu)], chunk(i, j, nu), 0))],
        out_specs=pl.BlockSpec((tm, d), lambda i, j, te, nu: (i, 0)),
        scratch_shapes=[pltpu.VMEM((tm, d), BF16), pltpu.VMEM((tm, d), F32)],
    )
    return pl.pallas_call(
        _gmm_body,
        grid_spec=grid_spec,
        out_shape=jax.ShapeDtypeStruct((n_slots, d), F32),
        compiler_params=pltpu.CompilerParams(
            dimension_semantics=("arbitrary", "arbitrary"), vmem_limit_bytes=VMEM_LIMIT),
        name="expert_ffn",
    )(tile_expert, n_used, xs, wg, wu, wd)


def _combine_body(dest_ref, h_ref, gate_ref, g_ref, y_ref, o_ref, buf_ref, sem, *, tm):
    def copy(r, k):
        return pltpu.make_async_copy(y_ref.at[pl.ds(dest_ref[0, 0, TOP_K * r + k], 1)],
                                     buf_ref.at[k, pl.ds(r, 1)], sem)

    def start(r, carry):
        for k in range(TOP_K):
            copy(r, k).start()
        return carry

    def wait(r, carry):
        for k in range(TOP_K):
            copy(r, k).wait()
        return carry

    lax.fori_loop(0, tm, start, 0)
    lax.fori_loop(0, tm, wait, 0)
    gate = gate_ref[...]
    h = h_ref[...] + gate[:, 0:1] * buf_ref[0] + gate[:, 1:2] * buf_ref[1]
    o_ref[...] = _rms(h, g_ref[...])


def _combine(h, gates, dest, y, g, *, batch, seq, lead):
    rows, d = h.shape
    tm = CHUNK
    lp = rows // batch
    nt, lead_t, lp_t = seq // tm, lead // tm, lp // tm
    dest3 = dest.reshape(rows // tm, 1, TOP_K * tm)
    in_row = lambda b, t: b * lp_t + lead_t + t
    return pl.pallas_call(
        functools.partial(_combine_body, tm=tm),
        grid=(batch, nt),
        in_specs=[pl.BlockSpec((1, 1, TOP_K * tm), lambda b, t: (in_row(b, t), 0, 0),
                               memory_space=pltpu.SMEM),
                  pl.BlockSpec((tm, d), lambda b, t: (in_row(b, t), 0)),
                  pl.BlockSpec((tm, LANES), lambda b, t: (in_row(b, t), 0)),
                  _const_spec((1, d)),
                  pl.BlockSpec(memory_space=pl.ANY)],
        out_specs=pl.BlockSpec((tm, d), lambda b, t: (b * nt + t, 0)),
        out_shape=jax.ShapeDtypeStruct((batch * seq, d), F32),
        scratch_shapes=[pltpu.VMEM((TOP_K, tm, d), F32), pltpu.SemaphoreType.DMA(())],
        compiler_params=pltpu.CompilerParams(
            dimension_semantics=("arbitrary", "arbitrary"), vmem_limit_bytes=VMEM_LIMIT),
        name="combine",
    )(dest3, h, gates, g, y)


def _row(v):
    return v.reshape(1, -1).astype(F32)


def _mix_layer(h, layer, p, *, batch, pad, heads, head_dim):
    width = heads * head_dim
    xw = p["ssd_conv_w"].shape[2]
    cw = p["cm_dw_w"].shape[2]
    w_in = p["w_in"][layer]
    o_dt = width + xw
    w_dt = w_in[:, o_dt:o_dt + heads]
    w_dt = jnp.pad(jnp.tile(w_dt, (1, DT_COPIES)), ((0, 0), (0, LANES - DT_COPIES * heads)))
    w_cat = jnp.concatenate([w_in[:, :o_dt], w_in[:, o_dt + heads:], w_dt], axis=1).astype(BF16)
    z, xbc, cm, dt = _in_proj(h, _row(p["norm_mix_g"][layer]), w_cat, (width, xw, 2 * cw, LANES))

    rep = lambda v: jnp.pad(jnp.tile(v.astype(F32), DT_COPIES), (0, LANES - DT_COPIES * heads))[None]
    r = jnp.arange(LANES)[:, None]
    col = jnp.arange(width)[None, :]
    e3 = ((r % heads == col // head_dim) & (r < DT_COPIES * heads)).astype(BF16)
    y_ssd = _ssd(xbc, dt, z, p["ssd_conv_w"][layer].astype(F32), _row(p["ssd_conv_b"][layer]),
                 rep(p["ssd_dt_bias"][layer]), rep(-jnp.exp(p["ssd_a_log"][layer].astype(F32))),
                 _row(jnp.repeat(p["ssd_d"][layer], head_dim)), _row(p["ssd_norm_g"][layer]), e3,
                 batch=batch, pad=pad, heads=heads, head_dim=head_dim)
    y_cm = _conformer(cm, _row(p["cm_pw_b"][layer]), p["cm_dw_w"][layer].astype(F32),
                      _row(p["cm_dw_b"][layer]), _row(p["cm_ln_g"][layer]), _row(p["cm_ln_b"][layer]),
                      batch=batch, pad=pad)
    return y_ssd, y_cm


def kernel(x, meta_tokens, norm_mix_g, w_in, ssd_conv_w, ssd_conv_b, ssd_dt_bias, ssd_a_log, ssd_d,
           ssd_norm_g, cm_pw_b, cm_dw_w, cm_dw_b, cm_ln_g, cm_ln_b, w_out, norm_ffn_g, ffn_w_gate,
           ffn_w_up, ffn_w_down, moe_router, moe_w_gate, moe_w_up, moe_w_down, final_norm_g):
    p = dict(norm_mix_g=norm_mix_g, w_in=w_in, ssd_conv_w=ssd_conv_w, ssd_conv_b=ssd_conv_b,
             ssd_dt_bias=ssd_dt_bias, ssd_a_log=ssd_a_log, ssd_d=ssd_d, ssd_norm_g=ssd_norm_g,
             cm_pw_b=cm_pw_b, cm_dw_w=cm_dw_w, cm_dw_b=cm_dw_b, cm_ln_g=cm_ln_g, cm_ln_b=cm_ln_b)
    batch, seq, d = x.shape
    n_meta = meta_tokens.shape[0]
    depth = w_in.shape[0]
    heads = ssd_a_log.shape[1]
    head_dim = ssd_norm_g.shape[1] // heads
    n_experts = moe_router.shape[2]
    pad = (-n_meta) % CHUNK
    lead = pad + n_meta
    assert depth == 2 and seq % CHUNK == 0 and heads * DT_COPIES <= LANES and n_experts <= LANES
    assert head_dim * 2 == LANES and heads % (2 * SSD_GROUPS) == 0
    lp = lead + seq
    rows = batch * lp

    meta = jnp.broadcast_to(meta_tokens[None].astype(x.dtype), (batch, n_meta, d))
    h = jnp.concatenate([jnp.zeros((batch, pad, d), x.dtype), meta, x], axis=1).reshape(rows, d)
    mix = functools.partial(_mix_layer, p=p, batch=batch, pad=pad, heads=heads, head_dim=head_dim)

    y_ssd, y_cm = mix(h, 0)
    h = _out_ffn(h, y_ssd, y_cm, w_out[0].astype(BF16), _row(norm_ffn_g[0]),
                 ffn_w_gate[0].astype(BF16), ffn_w_up[0].astype(BF16), ffn_w_down[0].astype(BF16))

    y_ssd, y_cm = mix(h, 1)
    w_router = jnp.pad(moe_router[0].astype(F32), ((0, 0), (0, LANES - n_experts)))
    h, v, idx, gates, counts = _out_router(h, y_ssd, y_cm, w_out[1].astype(BF16), _row(norm_ffn_g[1]),
                                           w_router, n_experts)
    tm_g = 512
    counts = counts[0, :n_experts].astype(jnp.int32)
    padded = (counts + tm_g - 1) // tm_g * tm_g
    pad_end = jnp.cumsum(padded)
    pad_start = pad_end - padded
    dest = pad_start[idx[:, :TOP_K]] + idx[:, TOP_K:2 * TOP_K]
    n_tiles = -(-(rows * TOP_K) // tm_g) + n_experts
    tile_expert = jnp.minimum(
        jnp.searchsorted(pad_end, jnp.arange(n_tiles, dtype=jnp.int32) * tm_g, side="right"),
        n_experts - 1).astype(jnp.int32)
    n_used = (pad_end[-1:] // tm_g).astype(jnp.int32)
    xs = _dispatch(v, dest.reshape(-1), n_tiles * tm_g)
    ys = _gmm(xs, tile_expert, n_used, moe_w_gate[0].astype(BF16), moe_w_up[0].astype(BF16),
              moe_w_down[0].astype(BF16), tm_g)
    out = _combine(h, gates, dest.reshape(-1), ys, _row(final_norm_g), batch=batch, seq=seq, lead=lead)
    return out.reshape(batch, seq, d)
```

```python
import functools

import jax
import jax.numpy as jnp
from jax import lax
from jax.experimental import pallas as pl
from jax.experimental.pallas import tpu as pltpu

F32 = jnp.float32
BF16 = jnp.bfloat16

CHUNK = 128
SSD_GROUPS = 2
SSD_STATE = 128
TOP_K = 2
RMS_EPS = 1e-6
LN_EPS = 1e-5
LANES = 128
SUBLANES = 8
DT_COPIES = 3
VMEM_LIMIT = 56 * 1024 * 1024


def _largest_divisor(n, candidates):
    for c in candidates:
        if n % c == 0:
            return c
    raise ValueError(f"no tile in {candidates} divides {n}")


def _const_spec(shape):
    nd = len(shape)
    return pl.BlockSpec(shape, lambda *_: (0,) * nd, pipeline_mode=pl.Buffered(1))


def _rms(x, g):
    return x * lax.rsqrt(jnp.mean(x * x, axis=-1, keepdims=True) + RMS_EPS) * g


def _silu(x):
    return x * jax.nn.sigmoid(x)


def _bdot(a, b):
    return jnp.dot(a, b, preferred_element_type=F32)


def _in_proj_body(h_ref, g_ref, w_ref, z_ref, xbc_ref, cm_ref, dt_ref, *, widths):
    wz, wx, wc, wd = widths
    u = _rms(h_ref[...], g_ref[...]).astype(BF16)
    o = 0
    z_ref[...] = _bdot(u, w_ref[:, o:o + wz]).astype(BF16)
    o += wz
    xbc_ref[...] = _bdot(u, w_ref[:, o:o + wx]).astype(BF16)
    o += wx
    cm_ref[...] = _bdot(u, w_ref[:, o:o + wc]).astype(BF16)
    o += wc
    dt_ref[...] = _bdot(u, w_ref[:, o:o + wd])


def _in_proj(h, g, w, widths):
    rows, d = h.shape
    tm = _largest_divisor(rows, (512, 256, 128))
    wz, wx, wc, wd = widths
    row_spec = lambda n: pl.BlockSpec((tm, n), lambda i: (i, 0))
    return pl.pallas_call(
        functools.partial(_in_proj_body, widths=widths),
        grid=(rows // tm,),
        in_specs=[row_spec(d), _const_spec((1, d)), _const_spec(w.shape)],
        out_specs=[row_spec(wz), row_spec(wx), row_spec(wc), row_spec(wd)],
        out_shape=[jax.ShapeDtypeStruct((rows, wz), BF16),
                   jax.ShapeDtypeStruct((rows, wx), BF16),
                   jax.ShapeDtypeStruct((rows, wc), BF16),
                   jax.ShapeDtypeStruct((rows, wd), F32)],
        compiler_params=pltpu.CompilerParams(
            dimension_semantics=("parallel",), vmem_limit_bytes=VMEM_LIMIT),
        name="in_proj",
    )(h, g, w)


def _ssd_body(xbc_ref, dt_ref, z_ref, cw_ref, cb_ref, dtb_ref, a_ref, dsk_ref, ng_ref, e3_ref,
              y_ref, carry_ref, state_ref, *, pad, heads, head_dim, conv_k):
    c = pl.program_id(1)
    width = heads * head_dim
    gw = width // SSD_GROUPS
    n = SSD_STATE

    @pl.when(c == 0)
    def _():
        carry_ref[...] = jnp.zeros_like(carry_ref)
        state_ref[...] = jnp.zeros_like(state_ref)

    row = c * CHUNK + lax.broadcasted_iota(jnp.int32, (CHUNK, 1), 0)
    live = row >= pad

    cur = jnp.where(live, xbc_ref[...].astype(F32), 0.0)
    ext = jnp.concatenate([carry_ref[...], cur], axis=0)
    carry_ref[...] = cur[CHUNK - SUBLANES:, :]
    acc = cb_ref[...] + cw_ref[conv_k - 1:conv_k, :] * cur
    for k in range(conv_k - 1):
        shifted = pltpu.roll(ext, conv_k - 1 - k, 0)[SUBLANES:, :]
        acc = acc + cw_ref[k:k + 1, :] * shifted
    xbc = _silu(acc)
    xs = xbc[:, :width]
    bm = xbc[:, width:width + SSD_GROUPS * n].astype(BF16)
    cm = xbc[:, width + SSD_GROUPS * n:].astype(BF16)

    dt = jnp.where(live, jax.nn.softplus(dt_ref[...] + dtb_ref[...]), 0.0)
    da = dt * a_ref[...]
    li = lax.broadcasted_iota(jnp.int32, (CHUNK, CHUNK), 0)
    si = lax.broadcasted_iota(jnp.int32, (CHUNK, CHUNK), 1)
    causal = li >= si
    cs = jnp.dot(causal.astype(F32), da, precision=lax.Precision.HIGHEST,
                 preferred_element_type=F32)
    cs_t = cs.T
    total = cs[CHUNK - 1:CHUNK, :]

    lane = lax.broadcasted_iota(jnp.int32, (1, LANES), 1)

    def expand(v):
        hi = v.astype(BF16).astype(F32)
        r1 = v - hi
        mid = r1.astype(BF16).astype(F32)
        lo = r1 - mid
        parts = jnp.where(lane < heads, hi,
                          jnp.where(lane < 2 * heads, mid,
                                    jnp.where(lane < 3 * heads, lo, 0.0)))
        return _bdot(parts.astype(BF16), e3_ref[...])

    dt_x = expand(dt)
    ecs_x = expand(jnp.exp(cs))
    dte_x = expand(jnp.exp(total - cs))
    cdec_x = expand(jnp.broadcast_to(jnp.exp(total), (SUBLANES, LANES)))[0:1, :]

    xdt = xs * dt_x
    xdt_b = xdt.astype(BF16)
    xend_b = (xdt * dte_x).astype(BF16)
    half = lax.broadcasted_iota(jnp.int32, (1, LANES), 1) < head_dim

    hpg = heads // SSD_GROUPS
    y_parts = []
    for g in range(SSD_GROUPS):
        cg = cm[:, g * n:(g + 1) * n]
        bg = bm[:, g * n:(g + 1) * n]
        cb = lax.dot_general(cg, bg, (((1,), (1,)), ((), ())), preferred_element_type=F32)
        for j in range(hpg // 2):
            ms = []
            for e in (g * hpg + 2 * j, g * hpg + 2 * j + 1):
                seg = cs[:, e:e + 1] - cs_t[e:e + 1, :]
                dec = jnp.exp(jnp.where(causal, seg, -jnp.inf))
                ms.append((cb * dec).astype(BF16))
            lhs = jnp.concatenate(ms, axis=1)
            lo_ch = (g * hpg + 2 * j) * head_dim
            x2 = xdt_b[:, lo_ch:lo_ch + LANES]
            zero = jnp.zeros_like(x2)
            rhs = jnp.concatenate([jnp.where(half, x2, zero), jnp.where(half, zero, x2)], axis=0)
            y_parts.append(_bdot(lhs, rhs))
    y = jnp.concatenate(y_parts, axis=1)

    off_parts = []
    for g in range(SSD_GROUPS):
        cg = cm[:, g * n:(g + 1) * n]
        bg = bm[:, g * n:(g + 1) * n]
        sg = state_ref[:, g * gw:(g + 1) * gw]
        off_parts.append(_bdot(cg, sg.astype(BF16)))
        st = lax.dot_general(bg, xend_b[:, g * gw:(g + 1) * gw], (((0,), (0,)), ((), ())),
                             preferred_element_type=F32)
        state_ref[:, g * gw:(g + 1) * gw] = sg * cdec_x[:, g * gw:(g + 1) * gw] + st
    y = y + jnp.concatenate(off_parts, axis=1) * ecs_x + xs * dsk_ref[...]

    y = y * _silu(z_ref[...].astype(F32))
    y_ref[...] = _rms(y, ng_ref[...]).astype(BF16)


def _ssd(xbc, dt, z, conv_w, conv_b, dt_bias, a, d_skip_x, norm_g, e3, *, batch, pad, heads, head_dim):
    rows, xw = xbc.shape
    width = heads * head_dim
    nc = rows // batch // CHUNK
    conv_k = conv_w.shape[0]
    blk = lambda n: pl.BlockSpec((CHUNK, n), lambda b, c: (b * nc + c, 0))
    return pl.pallas_call(
        functools.partial(_ssd_body, pad=pad, heads=heads, head_dim=head_dim, conv_k=conv_k),
        grid=(batch, nc),
        in_specs=[blk(xw), blk(LANES), blk(width),
                  _const_spec(conv_w.shape), _const_spec((1, xw)), _const_spec((1, LANES)),
                  _const_spec((1, LANES)), _const_spec((1, width)), _const_spec((1, width)),
                  _const_spec(e3.shape)],
        out_specs=blk(width),
        out_shape=jax.ShapeDtypeStruct((rows, width), BF16),
        scratch_shapes=[pltpu.VMEM((SUBLANES, xw), F32), pltpu.VMEM((SSD_STATE, width), F32)],
        compiler_params=pltpu.CompilerParams(
            dimension_semantics=("parallel", "arbitrary"), vmem_limit_bytes=VMEM_LIMIT),
        name="ssd",
    )(xbc, dt, z, conv_w, conv_b, dt_bias, a, d_skip_x, norm_g, e3)


def _cm_body(cm_ref, pwb_ref, w_ref, b_ref, lng_ref, lnb_ref, y_ref, carry_ref, *, pad, tm, conv_k, halo):
    t = pl.program_id(1)
    width = y_ref.shape[1]

    @pl.when(t == 0)
    def _():
        carry_ref[...] = jnp.zeros_like(carry_ref)

    row = t * tm + lax.broadcasted_iota(jnp.int32, (tm, 1), 0)
    a = cm_ref[...].astype(F32) + pwb_ref[...]
    glu = a[:, :width] * jax.nn.sigmoid(a[:, width:])
    glu = jnp.where(row >= pad, glu, 0.0)
    ext = jnp.concatenate([carry_ref[...], glu], axis=0)
    carry_ref[...] = glu[tm - halo:, :]

    acc = jnp.broadcast_to(b_ref[...], (tm, width))
    for r in range(SUBLANES):
        rolled = ext if r == 0 else pltpu.roll(ext, r, 0)
        for q in range((conv_k - 1 - r) // SUBLANES + 1):
            s = SUBLANES * q + r
            lo = halo - SUBLANES * q
            acc = acc + w_ref[conv_k - 1 - s:conv_k - s, :] * rolled[lo:lo + tm, :]

    mu = jnp.mean(acc, axis=-1, keepdims=True)
    cen = acc - mu
    y = cen * lax.rsqrt(jnp.mean(cen * cen, axis=-1, keepdims=True) + LN_EPS)
    y = y * lng_ref[...] + lnb_ref[...]
    y_ref[...] = _silu(y).astype(BF16)


def _conformer(cm, pw_b, dw_w, dw_b, ln_g, ln_b, *, batch, pad):
    rows, w2 = cm.shape
    width = w2 // 2
    conv_k = dw_w.shape[0]
    halo = -(-(conv_k - 1) // SUBLANES) * SUBLANES
    lp = rows // batch
    tm = _largest_divisor(lp, (128,))
    nt = lp // tm
    blk = lambda n: pl.BlockSpec((tm, n), lambda b, t: (b * nt + t, 0))
    return pl.pallas_call(
        functools.partial(_cm_body, pad=pad, tm=tm, conv_k=conv_k, halo=halo),
        grid=(batch, nt),
        in_specs=[blk(w2), _const_spec((1, w2)), _const_spec(dw_w.shape), _const_spec((1, width)),
                  _const_spec((1, width)), _const_spec((1, width))],
        out_specs=blk(width),
        out_shape=jax.ShapeDtypeStruct((rows, width), BF16),
        scratch_shapes=[pltpu.VMEM((halo, width), F32)],
        compiler_params=pltpu.CompilerParams(
            dimension_semantics=("parallel", "arbitrary"), vmem_limit_bytes=VMEM_LIMIT),
        name="conformer",
    )(cm, pw_b, dw_w, dw_b, ln_g, ln_b)


def _out_ffn_body(h_ref, ys_ref, yc_ref, wo_ref, g_ref, wg_ref, wu_ref, wd_ref, o_ref):
    ws = ys_ref.shape[1]
    h = h_ref[...] + _bdot(ys_ref[...], wo_ref[:ws, :]) + _bdot(yc_ref[...], wo_ref[ws:, :])
    v = _rms(h, g_ref[...]).astype(BF16)
    act = (_silu(_bdot(v, wg_ref[...])) * _bdot(v, wu_ref[...])).astype(BF16)
    o_ref[...] = h + _bdot(act, wd_ref[...])


def _out_ffn(h, ys, yc, w_out, g, wg, wu, wd):
    rows, d = h.shape
    tm = _largest_divisor(rows, (512, 256, 128))
    row_spec = lambda n: pl.BlockSpec((tm, n), lambda i: (i, 0))
    return pl.pallas_call(
        _out_ffn_body,
        grid=(rows // tm,),
        in_specs=[row_spec(d), row_spec(ys.shape[1]), row_spec(yc.shape[1]), _const_spec(w_out.shape),
                  _const_spec((1, d)), _const_spec(wg.shape), _const_spec(wu.shape), _const_spec(wd.shape)],
        out_specs=row_spec(d),
        out_shape=jax.ShapeDtypeStruct((rows, d), F32),
        compiler_params=pltpu.CompilerParams(
            dimension_semantics=("parallel",), vmem_limit_bytes=VMEM_LIMIT),
        name="out_ffn",
    )(h, ys, yc, w_out, g, wg, wu, wd)


def _out_router_body(h_ref, ys_ref, yc_ref, wo_ref, g_ref, wr_ref,
                     ho_ref, v_ref, idx_ref, gate_ref, cnt_ref, run_ref, *, n_experts):
    i = pl.program_id(0)
    tm = h_ref.shape[0]
    ws = ys_ref.shape[1]

    @pl.when(i == 0)
    def _():
        run_ref[...] = jnp.zeros_like(run_ref)

    h = h_ref[...] + _bdot(ys_ref[...], wo_ref[:ws, :]) + _bdot(yc_ref[...], wo_ref[ws:, :])
    ho_ref[...] = h
    v = _rms(h, g_ref[...])
    v_ref[...] = v

    logits = jnp.dot(v, wr_ref[...], precision=lax.Precision.HIGHEST, preferred_element_type=F32)
    lane = lax.broadcasted_iota(jnp.int32, (tm, LANES), 1)
    neg = -jnp.inf
    l1 = jnp.where(lane < n_experts, logits, neg)
    m1 = jnp.max(l1, axis=-1, keepdims=True)
    i1 = jnp.min(jnp.where(l1 == m1, lane, LANES), axis=-1, keepdims=True)
    l2 = jnp.where(lane == i1, neg, l1)
    m2 = jnp.max(l2, axis=-1, keepdims=True)
    i2 = jnp.min(jnp.where(l2 == m2, lane, LANES), axis=-1, keepdims=True)
    e21 = jnp.exp(m2 - m1)
    g1 = 1.0 / (1.0 + e21)
    g2 = e21 / (1.0 + e21)

    hot1 = lane == i1
    hot2 = lane == i2
    hot = jnp.where(hot1 | hot2, 1.0, 0.0)
    ri = lax.broadcasted_iota(jnp.int32, (tm, tm), 0)
    ci = lax.broadcasted_iota(jnp.int32, (tm, tm), 1)
    before = _bdot((ri > ci).astype(BF16), hot.astype(BF16)) + run_ref[...]
    p1 = jnp.sum(jnp.where(hot1, before, 0.0), axis=-1, keepdims=True).astype(jnp.int32)
    p2 = jnp.sum(jnp.where(hot2, before, 0.0), axis=-1, keepdims=True).astype(jnp.int32)
    run_ref[...] = run_ref[...] + jnp.sum(hot, axis=0, keepdims=True)
    cnt_ref[...] = jnp.broadcast_to(run_ref[...], cnt_ref.shape)

    idx_ref[...] = jnp.where(lane == 0, i1, jnp.where(lane == 1, i2, jnp.where(lane == 2, p1,
                             jnp.where(lane == 3, p2, 0))))
    gate_ref[...] = jnp.where(lane == 0, g1, jnp.where(lane == 1, g2, 0.0))


def _out_router(h, ys, yc, w_out, g, w_router, n_experts):
    rows, d = h.shape
    tm = _largest_divisor(rows, (512, 256, 128))
    row_spec = lambda n: pl.BlockSpec((tm, n), lambda i: (i, 0))
    return pl.pallas_call(
        functools.partial(_out_router_body, n_experts=n_experts),
        grid=(rows // tm,),
        in_specs=[row_spec(d), row_spec(ys.shape[1]), row_spec(yc.shape[1]), _const_spec(w_out.shape),
                  _const_spec((1, d)), _const_spec(w_router.shape)],
        out_specs=[row_spec(d), row_spec(d), row_spec(LANES), row_spec(LANES),
                   pl.BlockSpec((SUBLANES, LANES), lambda i: (0, 0))],
        out_shape=[jax.ShapeDtypeStruct((rows, d), F32), jax.ShapeDtypeStruct((rows, d), F32),
                   jax.ShapeDtypeStruct((rows, LANES), jnp.int32),
                   jax.ShapeDtypeStruct((rows, LANES), F32),
                   jax.ShapeDtypeStruct((SUBLANES, LANES), F32)],
        scratch_shapes=[pltpu.VMEM((1, LANES), F32)],
        compiler_params=pltpu.CompilerParams(
            dimension_semantics=("arbitrary",), vmem_limit_bytes=VMEM_LIMIT),
        name="out_router",
    )(h, ys, yc, w_out, g, w_router)


GATHER_UNROLL = 8


def _row_gather(src_ref, idx, n_rows, dst, sem):
    def copy(r):
        return pltpu.make_async_copy(src_ref.at[pl.ds(idx(r), 1)], dst(r), sem)

    def start():
        lax.fori_loop(0, n_rows, lambda r, c: (copy(r).start(), c)[1], 0, unroll=GATHER_UNROLL)

    def wait():
        lax.fori_loop(0, n_rows, lambda r, c: (copy(r).wait(), c)[1], 0, unroll=GATHER_UNROLL)

    return start, wait


def _gmm_body(te_ref, nu_ref, cur_ref, nxt_ref, v_ref, wg_ref, wu_ref, wd_ref, y_ref,
              xbuf_ref, xb_ref, acc_ref, sem):
    i = pl.program_id(0)
    j = pl.program_id(1)
    last = pl.num_programs(1) - 1
    tm = xb_ref.shape[0]
    n_used = nu_ref[0]

    def gather(idx_ref, slot):
        return _row_gather(v_ref, lambda r: idx_ref[0, 0, r], tm,
                           lambda r: xbuf_ref.at[slot, pl.ds(r, 1)], sem.at[slot])

    @pl.when(i < n_used)
    def _():
        @pl.when(j == 0)
        def _():
            slot = i % 2

            @pl.when(i == 0)
            def _():
                gather(cur_ref, 0)[0]()

            @pl.when(i + 1 < n_used)
            def _():
                gather(nxt_ref, 1 - slot)[0]()

            gather(cur_ref, slot)[1]()
            xb_ref[...] = xbuf_ref[slot].astype(BF16)
            acc_ref[...] = jnp.zeros_like(acc_ref)

        xb = xb_ref[...]
        act = (_silu(_bdot(xb, wg_ref[0])) * _bdot(xb, wu_ref[0])).astype(BF16)
        acc_ref[...] += _bdot(act, wd_ref[0])

        @pl.when(j == last)
        def _():
            y_ref[...] = acc_ref[...]

    @pl.when((i >= n_used) & (j == last))
    def _():
        y_ref[...] = jnp.zeros_like(y_ref)


def _gmm(v, row_tok, tile_expert, n_used, wg, wu, wd, tm):
    d = v.shape[1]
    n_slots = row_tok.shape[0]
    dff = wg.shape[2]
    fc = _largest_divisor(dff, (896, 512, 256, 128))
    nt = n_slots // tm
    nff = dff // fc
    rt3 = row_tok.reshape(nt, 1, tm)
    tile = lambda i, nu: jnp.minimum(i, nu[0] - 1)
    chunk = lambda i, j, nu: jnp.where(i < nu[0], j, nff - 1)
    idx_spec = lambda off: pl.BlockSpec((1, 1, tm), lambda i, j, te, nu: (tile(i + off, nu), 0, 0),
                                        memory_space=pltpu.SMEM)
    grid_spec = pltpu.PrefetchScalarGridSpec(
        num_scalar_prefetch=2,
        grid=(nt, nff),
        in_specs=[idx_spec(0), idx_spec(1), pl.BlockSpec(memory_space=pl.ANY),
                  pl.BlockSpec((1, d, fc), lambda i, j, te, nu: (te[tile(i, nu)], 0, chunk(i, j, nu))),
                  pl.BlockSpec((1, d, fc), lambda i, j, te, nu: (te[tile(i, nu)], 0, chunk(i, j, nu))),
                  pl.BlockSpec((1, fc, d), lambda i, j, te, nu: (te[tile(i, nu)], chunk(i, j, nu), 0))],
        out_specs=pl.BlockSpec((tm, d), lambda i, j, te, nu: (i, 0)),
        scratch_shapes=[pltpu.VMEM((2, tm, d), F32), pltpu.VMEM((tm, d), BF16), pltpu.VMEM((tm, d), F32),
                        pltpu.SemaphoreType.DMA((2,))],
    )
    return pl.pallas_call(
        _gmm_body,
        grid_spec=grid_spec,
        out_shape=jax.ShapeDtypeStruct((n_slots, d), F32),
        compiler_params=pltpu.CompilerParams(
            dimension_semantics=("arbitrary", "arbitrary"), vmem_limit_bytes=VMEM_LIMIT),
        name="expert_ffn",
    )(tile_expert, n_used, rt3, rt3, v, wg, wu, wd)


def _combine_body(cur_ref, nxt_ref, h_ref, gate_ref, g_ref, y_ref, o_ref, buf_ref, sem, *, tm):
    s = pl.program_id(0)
    slot = s % 2

    def gather(idx_ref, slot):
        return _row_gather(y_ref, lambda r: idx_ref[0, 0, r], TOP_K * tm,
                           lambda r: buf_ref.at[slot, pl.ds(r, 1)], sem.at[slot])

    @pl.when(s == 0)
    def _():
        gather(cur_ref, 0)[0]()

    @pl.when(s + 1 < pl.num_programs(0))
    def _():
        gather(nxt_ref, 1 - slot)[0]()

    gather(cur_ref, slot)[1]()
    gate = gate_ref[...]
    h = h_ref[...]
    for k in range(TOP_K):
        h = h + gate[:, k:k + 1] * buf_ref[slot, pl.ds(k * tm, tm), :]
    o_ref[...] = _rms(h, g_ref[...])


def _combine(h, gates, dest, y, g, *, batch, seq, lead):
    rows, d = h.shape
    tm = CHUNK
    lp = rows // batch
    nt, lead_t, lp_t = seq // tm, lead // tm, lp // tm
    n_steps = batch * nt
    dest3 = dest.reshape(rows // tm, tm, TOP_K).swapaxes(1, 2).reshape(rows // tm, 1, TOP_K * tm)
    in_row = lambda s: (s // nt) * lp_t + lead_t + s % nt
    nxt = lambda s: jnp.minimum(s + 1, n_steps - 1)
    idx_spec = lambda f: pl.BlockSpec((1, 1, TOP_K * tm), lambda s: (in_row(f(s)), 0, 0),
                                      memory_space=pltpu.SMEM)
    return pl.pallas_call(
        functools.partial(_combine_body, tm=tm),
        grid=(n_steps,),
        in_specs=[idx_spec(lambda s: s), idx_spec(nxt),
                  pl.BlockSpec((tm, d), lambda s: (in_row(s), 0)),
                  pl.BlockSpec((tm, LANES), lambda s: (in_row(s), 0)),
                  _const_spec((1, d)),
                  pl.BlockSpec(memory_space=pl.ANY)],
        out_specs=pl.BlockSpec((tm, d), lambda s: (s, 0)),
        out_shape=jax.ShapeDtypeStruct((batch * seq, d), F32),
        scratch_shapes=[pltpu.VMEM((2, TOP_K * tm, d), F32), pltpu.SemaphoreType.DMA((2,))],
        compiler_params=pltpu.CompilerParams(
            dimension_semantics=("arbitrary",), vmem_limit_bytes=VMEM_LIMIT),
        name="combine",
    )(dest3, dest3, h, gates, g, y)


def _row(v):
    return v.reshape(1, -1).astype(F32)


def _mix_layer(h, layer, p, *, batch, pad, heads, head_dim):
    width = heads * head_dim
    xw = p["ssd_conv_w"].shape[2]
    cw = p["cm_dw_w"].shape[2]
    w_in = p["w_in"][layer]
    o_dt = width + xw
    w_dt = w_in[:, o_dt:o_dt + heads]
    w_dt = jnp.pad(jnp.tile(w_dt, (1, DT_COPIES)), ((0, 0), (0, LANES - DT_COPIES * heads)))
    w_cat = jnp.concatenate([w_in[:, :o_dt], w_in[:, o_dt + heads:], w_dt], axis=1).astype(BF16)
    z, xbc, cm, dt = _in_proj(h, _row(p["norm_mix_g"][layer]), w_cat, (width, xw, 2 * cw, LANES))

    rep = lambda v: jnp.pad(jnp.tile(v.astype(F32), DT_COPIES), (0, LANES - DT_COPIES * heads))[None]
    r = jnp.arange(LANES)[:, None]
    col = jnp.arange(width)[None, :]
    e3 = ((r % heads == col // head_dim) & (r < DT_COPIES * heads)).astype(BF16)
    y_ssd = _ssd(xbc, dt, z, p["ssd_conv_w"][layer].astype(F32), _row(p["ssd_conv_b"][layer]),
                 rep(p["ssd_dt_bias"][layer]), rep(-jnp.exp(p["ssd_a_log"][layer].astype(F32))),
                 _row(jnp.repeat(p["ssd_d"][layer], head_dim)), _row(p["ssd_norm_g"][layer]), e3,
                 batch=batch, pad=pad, heads=heads, head_dim=head_dim)
    y_cm = _conformer(cm, _row(p["cm_pw_b"][layer]), p["cm_dw_w"][layer].astype(F32),
                      _row(p["cm_dw_b"][layer]), _row(p["cm_ln_g"][layer]), _row(p["cm_ln_b"][layer]),
                      batch=batch, pad=pad)
    return y_ssd, y_cm


def kernel(x, meta_tokens, norm_mix_g, w_in, ssd_conv_w, ssd_conv_b, ssd_dt_bias, ssd_a_log, ssd_d,
           ssd_norm_g, cm_pw_b, cm_dw_w, cm_dw_b, cm_ln_g, cm_ln_b, w_out, norm_ffn_g, ffn_w_gate,
           ffn_w_up, ffn_w_down, moe_router, moe_w_gate, moe_w_up, moe_w_down, final_norm_g):
    p = dict(norm_mix_g=norm_mix_g, w_in=w_in, ssd_conv_w=ssd_conv_w, ssd_conv_b=ssd_conv_b,
             ssd_dt_bias=ssd_dt_bias, ssd_a_log=ssd_a_log, ssd_d=ssd_d, ssd_norm_g=ssd_norm_g,
             cm_pw_b=cm_pw_b, cm_dw_w=cm_dw_w, cm_dw_b=cm_dw_b, cm_ln_g=cm_ln_g, cm_ln_b=cm_ln_b)
    batch, seq, d = x.shape
    n_meta = meta_tokens.shape[0]
    depth = w_in.shape[0]
    heads = ssd_a_log.shape[1]
    head_dim = ssd_norm_g.shape[1] // heads
    n_experts = moe_router.shape[2]
    pad = (-n_meta) % CHUNK
    lead = pad + n_meta
    assert depth == 2 and seq % CHUNK == 0 and heads * DT_COPIES <= LANES and n_experts <= LANES
    assert head_dim * 2 == LANES and heads % (2 * SSD_GROUPS) == 0
    lp = lead + seq
    rows = batch * lp

    meta = jnp.broadcast_to(meta_tokens[None].astype(x.dtype), (batch, n_meta, d))
    h = jnp.concatenate([jnp.zeros((batch, pad, d), x.dtype), meta, x], axis=1).reshape(rows, d)
    mix = functools.partial(_mix_layer, p=p, batch=batch, pad=pad, heads=heads, head_dim=head_dim)

    y_ssd, y_cm = mix(h, 0)
    h = _out_ffn(h, y_ssd, y_cm, w_out[0].astype(BF16), _row(norm_ffn_g[0]),
                 ffn_w_gate[0].astype(BF16), ffn_w_up[0].astype(BF16), ffn_w_down[0].astype(BF16))

    y_ssd, y_cm = mix(h, 1)
    w_router = jnp.pad(moe_router[0].astype(F32), ((0, 0), (0, LANES - n_experts)))
    h, v, idx, gates, counts = _out_router(h, y_ssd, y_cm, w_out[1].astype(BF16), _row(norm_ffn_g[1]),
                                           w_router, n_experts)
    tm_g = 512
    counts = counts[0, :n_experts].astype(jnp.int32)
    padded = (counts + tm_g - 1) // tm_g * tm_g
    pad_end = jnp.cumsum(padded)
    pad_start = pad_end - padded
    dest = pad_start[idx[:, :TOP_K]] + idx[:, TOP_K:2 * TOP_K]
    n_tiles = -(-(rows * TOP_K) // tm_g) + n_experts
    tile_expert = jnp.minimum(
        jnp.searchsorted(pad_end, jnp.arange(n_tiles, dtype=jnp.int32) * tm_g, side="right"),
        n_experts - 1).astype(jnp.int32)
    n_used = (pad_end[-1:] // tm_g).astype(jnp.int32)
    tok = jnp.repeat(jnp.arange(rows, dtype=jnp.int32), TOP_K)
    row_tok = jnp.zeros((n_tiles * tm_g,), jnp.int32).at[dest.reshape(-1)].set(tok)
    ys = _gmm(v, row_tok, tile_expert, n_used, moe_w_gate[0].astype(BF16), moe_w_up[0].astype(BF16),
              moe_w_down[0].astype(BF16), tm_g)
    out = _combine(h, gates, dest.reshape(-1), ys, _row(final_norm_g), batch=batch, seq=seq, lead=lead)
    return out.reshape(batch, seq, d)
```

```python
import functools

import jax
import jax.numpy as jnp
from jax import lax
from jax.experimental import pallas as pl
from jax.experimental.pallas import tpu as pltpu

F32 = jnp.float32
BF16 = jnp.bfloat16

CHUNK = 128
SSD_GROUPS = 2
SSD_STATE = 128
TOP_K = 2
RMS_EPS = 1e-6
LN_EPS = 1e-5
LANES = 128
SUBLANES = 8
DT_COPIES = 3
VMEM_LIMIT = 56 * 1024 * 1024


def _largest_divisor(n, candidates):
    for c in candidates:
        if n % c == 0:
            return c
    raise ValueError(f"no tile in {candidates} divides {n}")


def _const_spec(shape):
    nd = len(shape)
    return pl.BlockSpec(shape, lambda *_: (0,) * nd, pipeline_mode=pl.Buffered(1))


def _rms(x, g):
    return x * lax.rsqrt(jnp.mean(x * x, axis=-1, keepdims=True) + RMS_EPS) * g


def _silu(x):
    return x * jax.nn.sigmoid(x)


def _bdot(a, b):
    return jnp.dot(a, b, preferred_element_type=F32)


def _in_proj_body(h_ref, g_ref, w_ref, z_ref, xbc_ref, cm_ref, dt_ref, *, widths):
    wz, wx, wc, wd = widths
    u = _rms(h_ref[...], g_ref[...]).astype(BF16)
    o = 0
    z_ref[...] = _bdot(u, w_ref[:, o:o + wz]).astype(BF16)
    o += wz
    xbc_ref[...] = _bdot(u, w_ref[:, o:o + wx]).astype(BF16)
    o += wx
    cm_ref[...] = _bdot(u, w_ref[:, o:o + wc]).astype(BF16)
    o += wc
    dt_ref[...] = _bdot(u, w_ref[:, o:o + wd])


def _in_proj(h, g, w, widths):
    rows, d = h.shape
    tm = _largest_divisor(rows, (512, 256, 128))
    wz, wx, wc, wd = widths
    row_spec = lambda n: pl.BlockSpec((tm, n), lambda i: (i, 0))
    return pl.pallas_call(
        functools.partial(_in_proj_body, widths=widths),
        grid=(rows // tm,),
        in_specs=[row_spec(d), _const_spec((1, d)), _const_spec(w.shape)],
        out_specs=[row_spec(wz), row_spec(wx), row_spec(wc), row_spec(wd)],
        out_shape=[jax.ShapeDtypeStruct((rows, wz), BF16),
                   jax.ShapeDtypeStruct((rows, wx), BF16),
                   jax.ShapeDtypeStruct((rows, wc), BF16),
                   jax.ShapeDtypeStruct((rows, wd), F32)],
        compiler_params=pltpu.CompilerParams(
            dimension_semantics=("parallel",), vmem_limit_bytes=VMEM_LIMIT),
        name="in_proj",
    )(h, g, w)


def _ssd_body(xbc_ref, dt_ref, z_ref, cw_ref, cb_ref, dtb_ref, a_ref, dsk_ref, ng_ref, e3_ref,
              y_ref, carry_ref, state_ref, *, pad, heads, head_dim, conv_k):
    c = pl.program_id(1)
    width = heads * head_dim
    gw = width // SSD_GROUPS
    n = SSD_STATE

    @pl.when(c == 0)
    def _():
        carry_ref[...] = jnp.zeros_like(carry_ref)
        state_ref[...] = jnp.zeros_like(state_ref)

    row = c * CHUNK + lax.broadcasted_iota(jnp.int32, (CHUNK, 1), 0)
    live = row >= pad

    cur = jnp.where(live, xbc_ref[...].astype(F32), 0.0)
    ext = jnp.concatenate([carry_ref[...], cur], axis=0)
    carry_ref[...] = cur[CHUNK - SUBLANES:, :]
    acc = cb_ref[...] + cw_ref[conv_k - 1:conv_k, :] * cur
    for k in range(conv_k - 1):
        shifted = pltpu.roll(ext, conv_k - 1 - k, 0)[SUBLANES:, :]
        acc = acc + cw_ref[k:k + 1, :] * shifted
    xbc = _silu(acc)
    xs = xbc[:, :width]
    bm = xbc[:, width:width + SSD_GROUPS * n].astype(BF16)
    cm = xbc[:, width + SSD_GROUPS * n:].astype(BF16)

    dt = jnp.where(live, jax.nn.softplus(dt_ref[...] + dtb_ref[...]), 0.0)
    da = dt * a_ref[...]
    li = lax.broadcasted_iota(jnp.int32, (CHUNK, CHUNK), 0)
    si = lax.broadcasted_iota(jnp.int32, (CHUNK, CHUNK), 1)
    causal = li >= si
    cs = jnp.dot(causal.astype(F32), da, precision=lax.Precision.HIGHEST,
                 preferred_element_type=F32)
    cs_t = cs.T
    total = cs[CHUNK - 1:CHUNK, :]

    lane = lax.broadcasted_iota(jnp.int32, (1, LANES), 1)

    def expand(v):
        hi = v.astype(BF16).astype(F32)
        r1 = v - hi
        mid = r1.astype(BF16).astype(F32)
        lo = r1 - mid
        parts = jnp.where(lane < heads, hi,
                          jnp.where(lane < 2 * heads, mid,
                                    jnp.where(lane < 3 * heads, lo, 0.0)))
        return _bdot(parts.astype(BF16), e3_ref[...])

    dt_x = expand(dt)
    ecs_x = expand(jnp.exp(cs))
    dte_x = expand(jnp.exp(total - cs))
    cdec_x = expand(jnp.broadcast_to(jnp.exp(total), (SUBLANES, LANES)))[0:1, :]

    xdt = xs * dt_x
    xdt_b = xdt.astype(BF16)
    xend_b = (xdt * dte_x).astype(BF16)
    half = lax.broadcasted_iota(jnp.int32, (1, LANES), 1) < head_dim

    hpg = heads // SSD_GROUPS
    y_parts = []
    for g in range(SSD_GROUPS):
        cg = cm[:, g * n:(g + 1) * n]
        bg = bm[:, g * n:(g + 1) * n]
        cb = lax.dot_general(cg, bg, (((1,), (1,)), ((), ())), preferred_element_type=F32)
        for j in range(hpg // 2):
            ms = []
            for e in (g * hpg + 2 * j, g * hpg + 2 * j + 1):
                seg = cs[:, e:e + 1] - cs_t[e:e + 1, :]
                dec = jnp.exp(jnp.where(causal, seg, -jnp.inf))
                ms.append((cb * dec).astype(BF16))
            lhs = jnp.concatenate(ms, axis=1)
            lo_ch = (g * hpg + 2 * j) * head_dim
            x2 = xdt_b[:, lo_ch:lo_ch + LANES]
            zero = jnp.zeros_like(x2)
            rhs = jnp.concatenate([jnp.where(half, x2, zero), jnp.where(half, zero, x2)], axis=0)
            y_parts.append(_bdot(lhs, rhs))
    y = jnp.concatenate(y_parts, axis=1)

    off_parts = []
    for g in range(SSD_GROUPS):
        cg = cm[:, g * n:(g + 1) * n]
        bg = bm[:, g * n:(g + 1) * n]
        sg = state_ref[:, g * gw:(g + 1) * gw]
        off_parts.append(_bdot(cg, sg.astype(BF16)))
        st = lax.dot_general(bg, xend_b[:, g * gw:(g + 1) * gw], (((0,), (0,)), ((), ())),
                             preferred_element_type=F32)
        state_ref[:, g * gw:(g + 1) * gw] = sg * cdec_x[:, g * gw:(g + 1) * gw] + st
    y = y + jnp.concatenate(off_parts, axis=1) * ecs_x + xs * dsk_ref[...]

    y = y * _silu(z_ref[...].astype(F32))
    y_ref[...] = _rms(y, ng_ref[...]).astype(BF16)


def _ssd(xbc, dt, z, conv_w, conv_b, dt_bias, a, d_skip_x, norm_g, e3, *, batch, pad, heads, head_dim):
    rows, xw = xbc.shape
    width = heads * head_dim
    nc = rows // batch // CHUNK
    conv_k = conv_w.shape[0]
    blk = lambda n: pl.BlockSpec((CHUNK, n), lambda b, c: (b * nc + c, 0))
    return pl.pallas_call(
        functools.partial(_ssd_body, pad=pad, heads=heads, head_dim=head_dim, conv_k=conv_k),
        grid=(batch, nc),
        in_specs=[blk(xw), blk(LANES), blk(width),
                  _const_spec(conv_w.shape), _const_spec((1, xw)), _const_spec((1, LANES)),
                  _const_spec((1, LANES)), _const_spec((1, width)), _const_spec((1, width)),
                  _const_spec(e3.shape)],
        out_specs=blk(width),
        out_shape=jax.ShapeDtypeStruct((rows, width), BF16),
        scratch_shapes=[pltpu.VMEM((SUBLANES, xw), F32), pltpu.VMEM((SSD_STATE, width), F32)],
        compiler_params=pltpu.CompilerParams(
            dimension_semantics=("parallel", "arbitrary"), vmem_limit_bytes=VMEM_LIMIT),
        name="ssd",
    )(xbc, dt, z, conv_w, conv_b, dt_bias, a, d_skip_x, norm_g, e3)


def _cm_body(cm_ref, pwb_ref, w_ref, b_ref, lng_ref, lnb_ref, y_ref, carry_ref, *, pad, tm, conv_k, halo):
    t = pl.program_id(1)
    width = y_ref.shape[1]

    @pl.when(t == 0)
    def _():
        carry_ref[...] = jnp.zeros_like(carry_ref)

    row = t * tm + lax.broadcasted_iota(jnp.int32, (tm, 1), 0)
    a = cm_ref[...].astype(F32) + pwb_ref[...]
    glu = a[:, :width] * jax.nn.sigmoid(a[:, width:])
    glu = jnp.where(row >= pad, glu, 0.0)
    ext = jnp.concatenate([carry_ref[...], glu], axis=0)
    carry_ref[...] = glu[tm - halo:, :]

    acc = jnp.broadcast_to(b_ref[...], (tm, width))
    for r in range(SUBLANES):
        rolled = ext if r == 0 else pltpu.roll(ext, r, 0)
        for q in range((conv_k - 1 - r) // SUBLANES + 1):
            s = SUBLANES * q + r
            lo = halo - SUBLANES * q
            acc = acc + w_ref[conv_k - 1 - s:conv_k - s, :] * rolled[lo:lo + tm, :]

    mu = jnp.mean(acc, axis=-1, keepdims=True)
    cen = acc - mu
    y = cen * lax.rsqrt(jnp.mean(cen * cen, axis=-1, keepdims=True) + LN_EPS)
    y = y * lng_ref[...] + lnb_ref[...]
    y_ref[...] = _silu(y).astype(BF16)


def _conformer(cm, pw_b, dw_w, dw_b, ln_g, ln_b, *, batch, pad):
    rows, w2 = cm.shape
    width = w2 // 2
    conv_k = dw_w.shape[0]
    halo = -(-(conv_k - 1) // SUBLANES) * SUBLANES
    lp = rows // batch
    tm = _largest_divisor(lp, (128,))
    nt = lp // tm
    blk = lambda n: pl.BlockSpec((tm, n), lambda b, t: (b * nt + t, 0))
    return pl.pallas_call(
        functools.partial(_cm_body, pad=pad, tm=tm, conv_k=conv_k, halo=halo),
        grid=(batch, nt),
        in_specs=[blk(w2), _const_spec((1, w2)), _const_spec(dw_w.shape), _const_spec((1, width)),
                  _const_spec((1, width)), _const_spec((1, width))],
        out_specs=blk(width),
        out_shape=jax.ShapeDtypeStruct((rows, width), BF16),
        scratch_shapes=[pltpu.VMEM((halo, width), F32)],
        compiler_params=pltpu.CompilerParams(
            dimension_semantics=("parallel", "arbitrary"), vmem_limit_bytes=VMEM_LIMIT),
        name="conformer",
    )(cm, pw_b, dw_w, dw_b, ln_g, ln_b)


def _out_ffn_body(h_ref, ys_ref, yc_ref, wo_ref, g_ref, wg_ref, wu_ref, wd_ref, o_ref):
    ws = ys_ref.shape[1]
    h = h_ref[...] + _bdot(ys_ref[...], wo_ref[:ws, :]) + _bdot(yc_ref[...], wo_ref[ws:, :])
    v = _rms(h, g_ref[...]).astype(BF16)
    act = (_silu(_bdot(v, wg_ref[...])) * _bdot(v, wu_ref[...])).astype(BF16)
    o_ref[...] = h + _bdot(act, wd_ref[...])


def _out_ffn(h, ys, yc, w_out, g, wg, wu, wd):
    rows, d = h.shape
    tm = _largest_divisor(rows, (512, 256, 128))
    row_spec = lambda n: pl.BlockSpec((tm, n), lambda i: (i, 0))
    return pl.pallas_call(
        _out_ffn_body,
        grid=(rows // tm,),
        in_specs=[row_spec(d), row_spec(ys.shape[1]), row_spec(yc.shape[1]), _const_spec(w_out.shape),
                  _const_spec((1, d)), _const_spec(wg.shape), _const_spec(wu.shape), _const_spec(wd.shape)],
        out_specs=row_spec(d),
        out_shape=jax.ShapeDtypeStruct((rows, d), F32),
        compiler_params=pltpu.CompilerParams(
            dimension_semantics=("parallel",), vmem_limit_bytes=VMEM_LIMIT),
        name="out_ffn",
    )(h, ys, yc, w_out, g, wg, wu, wd)


def _out_router_body(h_ref, ys_ref, yc_ref, wo_ref, g_ref, wr_ref,
                     ho_ref, v_ref, idx_ref, gate_ref, cnt_ref, run_ref, *, n_experts):
    i = pl.program_id(0)
    tm = h_ref.shape[0]
    ws = ys_ref.shape[1]

    @pl.when(i == 0)
    def _():
        run_ref[...] = jnp.zeros_like(run_ref)

    h = h_ref[...] + _bdot(ys_ref[...], wo_ref[:ws, :]) + _bdot(yc_ref[...], wo_ref[ws:, :])
    ho_ref[...] = h
    v = _rms(h, g_ref[...])
    v_ref[...] = v

    v_hi = v.astype(BF16)
    v_lo = (v - v_hi.astype(F32)).astype(BF16)
    both = _bdot(v_hi, wr_ref[...]) + _bdot(v_lo, wr_ref[...])
    logits = both + pltpu.roll(both, LANES - n_experts, 1)
    lane = lax.broadcasted_iota(jnp.int32, (tm, LANES), 1)
    neg = -jnp.inf
    l1 = jnp.where(lane < n_experts, logits, neg)
    m1 = jnp.max(l1, axis=-1, keepdims=True)
    i1 = jnp.min(jnp.where(l1 == m1, lane, LANES), axis=-1, keepdims=True)
    l2 = jnp.where(lane == i1, neg, l1)
    m2 = jnp.max(l2, axis=-1, keepdims=True)
    i2 = jnp.min(jnp.where(l2 == m2, lane, LANES), axis=-1, keepdims=True)
    e21 = jnp.exp(m2 - m1)
    g1 = 1.0 / (1.0 + e21)
    g2 = e21 / (1.0 + e21)

    hot1 = lane == i1
    hot2 = lane == i2
    hot = jnp.where(hot1 | hot2, 1.0, 0.0)
    ri = lax.broadcasted_iota(jnp.int32, (tm, tm), 0)
    ci = lax.broadcasted_iota(jnp.int32, (tm, tm), 1)
    before = _bdot((ri > ci).astype(BF16), hot.astype(BF16)) + run_ref[...]
    p1 = jnp.sum(jnp.where(hot1, before, 0.0), axis=-1, keepdims=True).astype(jnp.int32)
    p2 = jnp.sum(jnp.where(hot2, before, 0.0), axis=-1, keepdims=True).astype(jnp.int32)
    run_ref[...] = run_ref[...] + jnp.sum(hot, axis=0, keepdims=True)
    cnt_ref[...] = jnp.broadcast_to(run_ref[...], cnt_ref.shape)

    idx_ref[...] = jnp.where(lane == 0, i1, jnp.where(lane == 1, i2, jnp.where(lane == 2, p1,
                             jnp.where(lane == 3, p2, 0))))
    gate_ref[...] = jnp.where(lane == 0, g1, jnp.where(lane == 1, g2, 0.0))


def _out_router(h, ys, yc, w_out, g, w_router, n_experts):
    rows, d = h.shape
    tm = _largest_divisor(rows, (512, 256, 128))
    row_spec = lambda n: pl.BlockSpec((tm, n), lambda i: (i, 0))
    return pl.pallas_call(
        functools.partial(_out_router_body, n_experts=n_experts),
        grid=(rows // tm,),
        in_specs=[row_spec(d), row_spec(ys.shape[1]), row_spec(yc.shape[1]), _const_spec(w_out.shape),
                  _const_spec((1, d)), _const_spec(w_router.shape)],
        out_specs=[row_spec(d), row_spec(d), row_spec(LANES), row_spec(LANES),
                   pl.BlockSpec((SUBLANES, LANES), lambda i: (0, 0))],
        out_shape=[jax.ShapeDtypeStruct((rows, d), F32), jax.ShapeDtypeStruct((rows, d), F32),
                   jax.ShapeDtypeStruct((rows, LANES), jnp.int32),
                   jax.ShapeDtypeStruct((rows, LANES), F32),
                   jax.ShapeDtypeStruct((SUBLANES, LANES), F32)],
        scratch_shapes=[pltpu.VMEM((1, LANES), F32)],
        compiler_params=pltpu.CompilerParams(
            dimension_semantics=("arbitrary",), vmem_limit_bytes=VMEM_LIMIT),
        name="out_router",
    )(h, ys, yc, w_out, g, w_router)


def _gmm_body(te_ref, nu_ref, cur_ref, nxt_ref, v_ref, wg_ref, wu_ref, wd_ref, y_ref,
              xbuf_ref, xb_ref, acc_ref, sem, *, nff):
    i = pl.program_id(0)
    j = pl.program_id(1)
    last = nff - 1
    tm = xb_ref.shape[0]
    share = tm // nff
    n_used = nu_ref[0]
    slot = i % 2

    def row_copy(idx_ref, r, slot):
        return pltpu.make_async_copy(v_ref.at[pl.ds(idx_ref[0, 0, r], 1)],
                                     xbuf_ref.at[slot, pl.ds(r, 1)], sem.at[slot])

    def tile_wait(slot):
        pltpu.make_async_copy(v_ref.at[pl.ds(0, tm)], xbuf_ref.at[slot], sem.at[slot]).wait()

    @pl.when(i < n_used)
    def _():
        @pl.when(j == 0)
        def _():
            @pl.when(i == 0)
            def _():
                lax.fori_loop(0, tm, lambda r, c: (row_copy(cur_ref, r, 0).start(), c)[1], 0,
                              unroll=SUBLANES)

            tile_wait(slot)
            xb_ref[...] = xbuf_ref[slot].astype(BF16)
            acc_ref[...] = jnp.zeros_like(acc_ref)

        for u in range(share):
            row_copy(nxt_ref, j * share + u, 1 - slot).start()

        xb = xb_ref[...]
        act = (_silu(_bdot(xb, wg_ref[0])) * _bdot(xb, wu_ref[0])).astype(BF16)
        acc_ref[...] += _bdot(act, wd_ref[0])

        @pl.when(j == last)
        def _():
            y_ref[...] = acc_ref[...]

        @pl.when((j == last) & (i == n_used - 1))
        def _():
            tile_wait(1 - slot)

    @pl.when((i >= n_used) & (j == last))
    def _():
        y_ref[...] = jnp.zeros_like(y_ref)


def _gmm(v, row_tok, tile_expert, n_used, wg, wu, wd, tm):
    d = v.shape[1]
    n_slots = row_tok.shape[0]
    dff = wg.shape[2]
    fc = _largest_divisor(dff, (1792, 1024, 512, 256, 128))
    nt = n_slots // tm
    nff = dff // fc
    rt3 = row_tok.reshape(nt, 1, tm)
    tile = lambda i, nu: jnp.maximum(jnp.minimum(i, nu[0] - 1), 0)
    chunk = lambda i, j, nu: jnp.where(i < nu[0], j, nff - 1)
    idx_spec = lambda off: pl.BlockSpec((1, 1, tm), lambda i, j, te, nu: (tile(i + off, nu), 0, 0),
                                        memory_space=pltpu.SMEM)
    grid_spec = pltpu.PrefetchScalarGridSpec(
        num_scalar_prefetch=2,
        grid=(nt, nff),
        in_specs=[idx_spec(0), idx_spec(1), pl.BlockSpec(memory_space=pl.ANY),
                  pl.BlockSpec((1, d, fc), lambda i, j, te, nu: (te[tile(i, nu)], 0, chunk(i, j, nu))),
                  pl.BlockSpec((1, d, fc), lambda i, j, te, nu: (te[tile(i, nu)], 0, chunk(i, j, nu))),
                  pl.BlockSpec((1, fc, d), lambda i, j, te, nu: (te[tile(i, nu)], chunk(i, j, nu), 0))],
        out_specs=pl.BlockSpec((tm, d), lambda i, j, te, nu: (i, 0)),
        scratch_shapes=[pltpu.VMEM((2, tm, d), F32), pltpu.VMEM((tm, d), BF16), pltpu.VMEM((tm, d), F32),
                        pltpu.SemaphoreType.DMA((2,))],
    )
    return pl.pallas_call(
        functools.partial(_gmm_body, nff=nff),
        grid_spec=grid_spec,
        out_shape=jax.ShapeDtypeStruct((n_slots, d), F32),
        compiler_params=pltpu.CompilerParams(
            dimension_semantics=("arbitrary", "arbitrary"), vmem_limit_bytes=VMEM_LIMIT),
        name="expert_ffn",
    )(tile_expert, n_used, rt3, rt3, v, wg, wu, wd)


def _combine_body(cur_ref, nxt_ref, h_ref, gate_ref, g_ref, y_ref, o_ref, buf_ref, sem, *, tm):
    s = pl.program_id(0)
    slot = s % 2
    groups = TOP_K * tm // SUBLANES

    def start_tile(idx_ref, slot):
        def group(q, c):
            for u in range(SUBLANES):
                src = idx_ref[0, 0, q * SUBLANES + u]
                pltpu.make_async_copy(y_ref.at[src >> 3, pl.ds(src & (SUBLANES - 1), 1)],
                                      buf_ref.at[slot, q, pl.ds(u, 1)], sem.at[slot]).start()
            return c
        lax.fori_loop(0, groups, group, 0)

    @pl.when(s == 0)
    def _():
        start_tile(cur_ref, 0)

    @pl.when(s + 1 < pl.num_programs(0))
    def _():
        start_tile(nxt_ref, 1 - slot)

    pltpu.make_async_copy(y_ref.at[pl.ds(0, groups)], buf_ref.at[slot], sem.at[slot]).wait()
    gate = gate_ref[...]
    h = h_ref[...]
    for k in range(TOP_K):
        rows = buf_ref[slot, pl.ds(k * tm // SUBLANES, tm // SUBLANES)]
        h = h + gate[:, k:k + 1] * rows.reshape(tm, rows.shape[-1])
    o_ref[...] = _rms(h, g_ref[...])


def _combine(h, gates, dest, y, g, *, batch, seq, lead):
    rows, d = h.shape
    tm = CHUNK
    lp = rows // batch
    nt, lead_t, lp_t = seq // tm, lead // tm, lp // tm
    n_steps = batch * nt
    dest3 = dest.reshape(rows // tm, tm, TOP_K).swapaxes(1, 2).reshape(rows // tm, 1, TOP_K * tm)
    in_row = lambda s: (s // nt) * lp_t + lead_t + s % nt
    nxt = lambda s: jnp.minimum(s + 1, n_steps - 1)
    idx_spec = lambda f: pl.BlockSpec((1, 1, TOP_K * tm), lambda s: (in_row(f(s)), 0, 0),
                                      memory_space=pltpu.SMEM)
    return pl.pallas_call(
        functools.partial(_combine_body, tm=tm),
        grid=(n_steps,),
        in_specs=[idx_spec(lambda s: s), idx_spec(nxt),
                  pl.BlockSpec((tm, d), lambda s: (in_row(s), 0)),
                  pl.BlockSpec((tm, LANES), lambda s: (in_row(s), 0)),
                  _const_spec((1, d)),
                  pl.BlockSpec(memory_space=pl.ANY)],
        out_specs=pl.BlockSpec((tm, d), lambda s: (s, 0)),
        out_shape=jax.ShapeDtypeStruct((batch * seq, d), F32),
        scratch_shapes=[pltpu.VMEM((2, TOP_K * tm // SUBLANES, SUBLANES, d), F32),
                        pltpu.SemaphoreType.DMA((2,))],
        compiler_params=pltpu.CompilerParams(
            dimension_semantics=("arbitrary",), vmem_limit_bytes=VMEM_LIMIT),
        name="combine",
    )(dest3, dest3, h, gates, g, y.reshape(-1, SUBLANES, d))


def _row(v):
    return v.reshape(1, -1).astype(F32)


def _mix_layer(h, layer, p, *, batch, pad, heads, head_dim):
    width = heads * head_dim
    xw = p["ssd_conv_w"].shape[2]
    cw = p["cm_dw_w"].shape[2]
    w_in = p["w_in"][layer]
    o_dt = width + xw
    w_dt = w_in[:, o_dt:o_dt + heads]
    w_dt = jnp.pad(jnp.tile(w_dt, (1, DT_COPIES)), ((0, 0), (0, LANES - DT_COPIES * heads)))
    w_cat = jnp.concatenate([w_in[:, :o_dt], w_in[:, o_dt + heads:], w_dt], axis=1).astype(BF16)
    z, xbc, cm, dt = _in_proj(h, _row(p["norm_mix_g"][layer]), w_cat, (width, xw, 2 * cw, LANES))

    rep = lambda v: jnp.pad(jnp.tile(v.astype(F32), DT_COPIES), (0, LANES - DT_COPIES * heads))[None]
    r = jnp.arange(LANES)[:, None]
    col = jnp.arange(width)[None, :]
    e3 = ((r % heads == col // head_dim) & (r < DT_COPIES * heads)).astype(BF16)
    y_ssd = _ssd(xbc, dt, z, p["ssd_conv_w"][layer].astype(F32), _row(p["ssd_conv_b"][layer]),
                 rep(p["ssd_dt_bias"][layer]), rep(-jnp.exp(p["ssd_a_log"][layer].astype(F32))),
                 _row(jnp.repeat(p["ssd_d"][layer], head_dim)), _row(p["ssd_norm_g"][layer]), e3,
                 batch=batch, pad=pad, heads=heads, head_dim=head_dim)
    y_cm = _conformer(cm, _row(p["cm_pw_b"][layer]), p["cm_dw_w"][layer].astype(F32),
                      _row(p["cm_dw_b"][layer]), _row(p["cm_ln_g"][layer]), _row(p["cm_ln_b"][layer]),
                      batch=batch, pad=pad)
    return y_ssd, y_cm


def kernel(x, meta_tokens, norm_mix_g, w_in, ssd_conv_w, ssd_conv_b, ssd_dt_bias, ssd_a_log, ssd_d,
           ssd_norm_g, cm_pw_b, cm_dw_w, cm_dw_b, cm_ln_g, cm_ln_b, w_out, norm_ffn_g, ffn_w_gate,
           ffn_w_up, ffn_w_down, moe_router, moe_w_gate, moe_w_up, moe_w_down, final_norm_g):
    p = dict(norm_mix_g=norm_mix_g, w_in=w_in, ssd_conv_w=ssd_conv_w, ssd_conv_b=ssd_conv_b,
             ssd_dt_bias=ssd_dt_bias, ssd_a_log=ssd_a_log, ssd_d=ssd_d, ssd_norm_g=ssd_norm_g,
             cm_pw_b=cm_pw_b, cm_dw_w=cm_dw_w, cm_dw_b=cm_dw_b, cm_ln_g=cm_ln_g, cm_ln_b=cm_ln_b)
    batch, seq, d = x.shape
    n_meta = meta_tokens.shape[0]
    depth = w_in.shape[0]
    heads = ssd_a_log.shape[1]
    head_dim = ssd_norm_g.shape[1] // heads
    n_experts = moe_router.shape[2]
    pad = (-n_meta) % CHUNK
    lead = pad + n_meta
    assert depth == 2 and seq % CHUNK == 0 and heads * DT_COPIES <= LANES and n_experts <= LANES
    assert head_dim * 2 == LANES and heads % (2 * SSD_GROUPS) == 0
    lp = lead + seq
    rows = batch * lp

    meta = jnp.broadcast_to(meta_tokens[None].astype(x.dtype), (batch, n_meta, d))
    h = jnp.concatenate([jnp.zeros((batch, pad, d), x.dtype), meta, x], axis=1).reshape(rows, d)
    mix = functools.partial(_mix_layer, p=p, batch=batch, pad=pad, heads=heads, head_dim=head_dim)

    y_ssd, y_cm = mix(h, 0)
    h = _out_ffn(h, y_ssd, y_cm, w_out[0].astype(BF16), _row(norm_ffn_g[0]),
                 ffn_w_gate[0].astype(BF16), ffn_w_up[0].astype(BF16), ffn_w_down[0].astype(BF16))

    y_ssd, y_cm = mix(h, 1)
    wr = moe_router[0].astype(F32)
    wr_hi = wr.astype(BF16)
    wr_lo = (wr - wr_hi.astype(F32)).astype(BF16)
    w_router = jnp.pad(jnp.concatenate([wr_hi, wr_lo], axis=1), ((0, 0), (0, LANES - 2 * n_experts)))
    h, v, idx, gates, counts = _out_router(h, y_ssd, y_cm, w_out[1].astype(BF16), _row(norm_ffn_g[1]),
                                           w_router, n_experts)
    tm_g = 512
    counts = counts[0, :n_experts].astype(jnp.int32)
    padded = (counts + tm_g - 1) // tm_g * tm_g
    pad_end = jnp.cumsum(padded)
    pad_start = pad_end - padded
    dest = pad_start[idx[:, :TOP_K]] + idx[:, TOP_K:2 * TOP_K]
    n_tiles = -(-(rows * TOP_K) // tm_g) + n_experts
    tile_expert = jnp.minimum(
        jnp.searchsorted(pad_end, jnp.arange(n_tiles, dtype=jnp.int32) * tm_g, side="right"),
        n_experts - 1).astype(jnp.int32)
    n_used = (pad_end[-1:] // tm_g).astype(jnp.int32)
    tok = jnp.repeat(jnp.arange(rows, dtype=jnp.int32), TOP_K)
    row_tok = jnp.zeros((n_tiles * tm_g,), jnp.int32).at[dest.reshape(-1)].set(
        tok, unique_indices=True)
    ys = _gmm(v, row_tok, tile_expert, n_used, moe_w_gate[0].astype(BF16), moe_w_up[0].astype(BF16),
              moe_w_down[0].astype(BF16), tm_g)
    out = _combine(h, gates, dest.reshape(-1), ys, _row(final_norm_g), batch=batch, seq=seq, lead=lead)
    return out.reshape(batch, seq, d)
```
